```python
import math
import jax, jax.numpy as jnp
from jax import lax
import numpy as np

D_MODEL = 1024
BATCH = 8
SEQ = 8192
DEPTH = 2

GRID_W = 64
CTX_LEN = 256

GDN_HEAD_DIM = 128
GDN_WIDTH = D_MODEL // 2
GDN_HEADS = GDN_WIDTH // GDN_HEAD_DIM
GDN_CHUNK = 64
CONV_K = 5
GMLP_WIDTH = D_MODEL - GDN_WIDTH
GMLP_GROUP_DIM = 128
GMLP_GROUPS = GMLP_WIDTH // GMLP_GROUP_DIM
GMLP_CHUNK = 128
GDN_PROJ = 4 * GDN_WIDTH + 4 * GDN_HEADS
PROJ_WIDTH = GDN_PROJ + 2 * GMLP_WIDTH
MIX_WIDTH = GDN_WIDTH + GMLP_WIDTH
D_FF = 256 * ((8 * D_MODEL // 3 + 255) // 256)
N_EXPERTS = 8
TOP_K = 2
D_FF_EXPERT = 7 * D_MODEL // 2
MOE_BLOCK = 512
N_DENSE = (DEPTH + 1) // 2
N_MOE = DEPTH // 2
EPS = 1e-6

kernel_name = "hybrid_gdn_gmlp_moe_dit"

F32 = jnp.float32


def rms_norm(x, g):
    xf = x.astype(F32)
    y = xf * lax.rsqrt(jnp.mean(xf * xf, axis=-1, keepdims=True) + EPS)
    return (y * g.astype(F32)).astype(x.dtype)


def layer_norm(x, g):
    xf = x.astype(F32)
    mu = jnp.mean(xf, axis=-1, keepdims=True)
    xc = xf - mu
    y = xc * lax.rsqrt(jnp.mean(xc * xc, axis=-1, keepdims=True) + EPS)
    return (y * g.astype(F32)).astype(x.dtype)


def l2_normalize(x):
    return x * lax.rsqrt(jnp.sum(x * x, axis=-1, keepdims=True) + EPS)


def modulate(h, shift, scale):
    return h * (1 + scale) + shift


def grid_pos_embed(rows, dim):
    row = jnp.broadcast_to(jnp.arange(rows, dtype=F32)[:, None], (rows, GRID_W)).reshape(-1)
    col = jnp.broadcast_to(jnp.arange(GRID_W, dtype=F32)[None, :], (rows, GRID_W)).reshape(-1)
    quarter = dim // 4
    omega = 1.0 / (10000.0 ** (jnp.arange(quarter, dtype=F32) / quarter))

    def enc(p):
        ang = p[:, None] * omega[None, :]
        return jnp.concatenate([jnp.sin(ang), jnp.cos(ang)], axis=-1)

    return jnp.concatenate([enc(row), enc(col)], axis=-1)


def short_conv(x, w):
    ch = x.shape[-1]
    y = lax.conv_general_dilated(
        x, w[:, None, :].astype(x.dtype), window_strides=(1,),
        padding=[(CONV_K // 2, CONV_K // 2)],
        dimension_numbers=("NWC", "WIO", "NWC"), feature_group_count=ch)
    return jax.nn.silu(y)


def gdn_chunk_scan(q, k, v, g, beta, s0):
    bsz, seq, nh, dh = q.shape
    n = seq // GDN_CHUNK
    c = GDN_CHUNK

    def to_chunks(t):
        t = t.reshape((bsz, n, c, nh) + t.shape[3:])
        return jnp.moveaxis(jnp.moveaxis(t, 3, 2), 1, 0)

    q = to_chunks(q) * (dh ** -0.5)
    k = to_chunks(k)
    v = to_chunks(v)
    beta = to_chunks(beta)
    gc = jnp.cumsum(to_chunks(g), axis=-1)
    incl = jnp.tril(jnp.ones((c, c), bool))
    strict = jnp.tril(jnp.ones((c, c), bool), -1)
    diff = gc[..., :, None] - gc[..., None, :]
    gamma = jnp.where(incl, jnp.exp(jnp.where(incl, diff, 0.0)), 0.0)
    kb = k * beta[..., None]
    a = jnp.where(strict, jnp.einsum("nbhik,nbhjk->nbhij", kb, k) * gamma, 0.0)
    eye = jnp.eye(c, dtype=a.dtype)
    t_inv = lax.linalg.triangular_solve(eye + a, jnp.broadcast_to(eye, a.shape),
                                        left_side=True, lower=True, unit_diagonal=True)
    u = jnp.einsum("nbhij,nbhjd->nbhid", t_inv, v * beta[..., None])
    w = jnp.einsum("nbhij,nbhjd->nbhid", t_inv, kb * jnp.exp(gc)[..., None])
    qk = jnp.where(incl, jnp.einsum("nbhik,nbhjk->nbhij", q, k) * gamma, 0.0)
    q_dec = q * jnp.exp(gc)[..., None]
    k_dec = k * jnp.exp(gc[..., -1:] - gc)[..., None]
    g_last = jnp.exp(gc[..., -1])

    def step(s, inp):
        q_c, k_c, u_c, w_c, qk_c, gl = inp
        v_new = u_c - jnp.einsum("bhck,bhkv->bhcv", w_c, s)
        o = jnp.einsum("bhck,bhkv->bhcv", q_c, s) + jnp.einsum("bhij,bhjv->bhiv", qk_c, v_new)
        s = s * gl[..., None, None] + jnp.einsum("bhck,bhcv->bhkv", k_c, v_new)
        return s, o

    s, o = lax.scan(step, s0, (q_dec, k_dec, u, w, qk, g_last))
    o = jnp.moveaxis(jnp.moveaxis(o, 0, 1), 2, 3).reshape(bsz, seq, nh, dh)
    return o, s


def gdn_branch(p, conv_w, a_log, dt_bias, o_norm_g, s_f0, s_b0):
    bsz, seq, _ = p.shape
    qkv = short_conv(p[..., :3 * GDN_WIDTH], conv_w).astype(F32)
    qkv = qkv.reshape(bsz, seq, 3, GDN_HEADS, GDN_HEAD_DIM)
    q = l2_normalize(qkv[:, :, 0])
    k = l2_normalize(qkv[:, :, 1])
    v = qkv[:, :, 2]
    z = p[..., 3 * GDN_WIDTH:4 * GDN_WIDTH].reshape(bsz, seq, GDN_HEADS, GDN_HEAD_DIM)
    gates = p[..., 4 * GDN_WIDTH:GDN_PROJ].astype(F32).reshape(bsz, seq, 2, 2, GDN_HEADS)
    decay = -jnp.exp(a_log.astype(F32)) * jax.nn.softplus(gates[:, :, :, 0] + dt_bias.astype(F32))
    beta = jax.nn.sigmoid(gates[:, :, :, 1])
    o_f, s_f = gdn_chunk_scan(q, k, v, decay[:, :, 0], beta[:, :, 0], s_f0)
    rev = lambda t: jnp.flip(t, axis=1)
    o_b, s_b = gdn_chunk_scan(rev(q), rev(k), rev(v), rev(decay[:, :, 1]), rev(beta[:, :, 1]), s_b0)
    o = o_f + rev(o_b)
    o = rms_norm(o, o_norm_g) * jax.nn.silu(z.astype(F32))
    return o.reshape(bsz, seq, GDN_WIDTH).astype(p.dtype), s_f, s_b


def gmlp_branch(p, sgu_norm_g, w_s, b_s):
    bsz, seq, _ = p.shape
    gu = jax.nn.gelu(p[..., GDN_PROJ:GDN_PROJ + GMLP_WIDTH])
    gv = layer_norm(jax.nn.gelu(p[..., GDN_PROJ + GMLP_WIDTH:]), sgu_norm_g)
    gv = gv.reshape(bsz, seq // GMLP_CHUNK, GMLP_CHUNK, GMLP_GROUPS, GMLP_GROUP_DIM)
    mixed = jnp.einsum("gpq,bnqgc->bnpgc", w_s, gv) + jnp.swapaxes(b_s, 0, 1)[None, None, :, :, None]
    return gu * mixed.reshape(bsz, seq, GMLP_WIDTH)


def swiglu(h, w1, w3, w2):
    return (jax.nn.silu(h @ w1) * (h @ w3)) @ w2


def moe_swiglu(h, router_w, router_b, w1, w3, w2):
    n_tok, dim = h.shape
    n_assign = n_tok * TOP_K
    logits = h.astype(F32) @ router_w.astype(F32) + router_b.astype(F32)
    probs = jax.nn.softmax(logits, axis=-1)
    top_p, top_e = lax.top_k(probs, TOP_K)
    top_p = top_p / jnp.sum(top_p, axis=-1, keepdims=True)
    flat_e = top_e.reshape(-1)
    flat_tok = jnp.arange(n_assign, dtype=jnp.int32) // TOP_K
    order = jnp.argsort(flat_e)
    sorted_e = flat_e[order]
    sorted_tok = flat_tok[order]
    sorted_p = top_p.reshape(-1)[order]
    counts = jnp.bincount(flat_e, length=N_EXPERTS)
    padded = (counts + MOE_BLOCK - 1) // MOE_BLOCK * MOE_BLOCK
    seg_end = jnp.cumsum(padded)
    pad_start = seg_end - padded
    start = jnp.cumsum(counts) - counts
    dest = pad_start[sorted_e] + jnp.arange(n_assign, dtype=jnp.int32) - start[sorted_e]
    n_blocks = -(-n_assign // MOE_BLOCK) + N_EXPERTS
    buf_tok = jnp.zeros((n_blocks * MOE_BLOCK,), jnp.int32).at[dest].set(sorted_tok)
    block_e = jnp.minimum(
        jnp.searchsorted(seg_end, jnp.arange(n_blocks, dtype=jnp.int32) * MOE_BLOCK, side="right"),
        N_EXPERTS - 1)
    xb = h[buf_tok].reshape(n_blocks, MOE_BLOCK, dim)

    def expert_block(args):
        xe, e = args
        return swiglu(xe, w1[e], w3[e], w2[e])

    yb = lax.map(expert_block, (xb, block_e)).reshape(-1, dim)
    y = yb[dest] * sorted_p[:, None].astype(h.dtype)
    return jnp.zeros_like(h).at[sorted_tok].add(y)


def setup_inputs(seed: int = 0) -> dict:
    key = jax.random.key(seed)
    ks = iter(jax.random.split(key, 32))
    nrm = lambda shape, s: jax.random.normal(next(ks), shape, F32) * s
    gain = lambda shape: 1.0 + nrm(shape, 0.02)
    d = D_MODEL
    a_init = jax.random.uniform(next(ks), (DEPTH, 2, GDN_HEADS), F32, 1.0, 16.0)
    dt = jnp.exp(jax.random.uniform(next(ks), (DEPTH, 2, GDN_HEADS), F32, math.log(1e-3), math.log(1e-1)))
    return {
        "x": nrm((BATCH, SEQ, d), 1.0),
        "c": nrm((BATCH, d), 1.0),
        "ctx": nrm((BATCH, CTX_LEN, d), 1.0),
        "c_ctx": nrm((d,), 1.0),
        "w_mod": nrm((DEPTH, d, 6 * d), 0.5 * d ** -0.5),
        "b_mod": nrm((DEPTH, 6 * d), 0.02),
        "norm1_g": gain((DEPTH, d)),
        "norm2_g": gain((DEPTH, d)),
        "w_in": nrm((DEPTH, d, PROJ_WIDTH), d ** -0.5),
        "conv_w": nrm((DEPTH, CONV_K, 3 * GDN_WIDTH), CONV_K ** -0.5),
        "a_log": jnp.log(a_init),
        "dt_bias": dt + jnp.log(-jnp.expm1(-dt)),
        "o_norm_g": gain((DEPTH, GDN_HEAD_DIM)),
        "sgu_norm_g": gain((DEPTH, GMLP_WIDTH)),
        "w_s": nrm((DEPTH, GMLP_GROUPS, GMLP_CHUNK, GMLP_CHUNK), 0.5 * GMLP_CHUNK ** -0.5),
        "b_s": gain((DEPTH, GMLP_GROUPS, GMLP_CHUNK)),
        "w_out": nrm((DEPTH, MIX_WIDTH, d), MIX_WIDTH ** -0.5),
        "ffn_w1": nrm((N_DENSE, d, D_FF), d ** -0.5),
        "ffn_w3": nrm((N_DENSE, d, D_FF), d ** -0.5),
        "ffn_w2": nrm((N_DENSE, D_FF, d), D_FF ** -0.5),
        "router_w": nrm((N_MOE, d, N_EXPERTS), d ** -0.5),
        "router_b": nrm((N_MOE, N_EXPERTS), 0.01),
        "exp_w1": nrm((N_MOE, N_EXPERTS, d, D_FF_EXPERT), d ** -0.5),
        "exp_w3": nrm((N_MOE, N_EXPERTS, d, D_FF_EXPERT), d ** -0.5),
        "exp_w2": nrm((N_MOE, N_EXPERTS, D_FF_EXPERT, d), D_FF_EXPERT ** -0.5),
        "final_g": gain((d,)),
    }


def reference(x, c, ctx, c_ctx, w_mod, b_mod, norm1_g, norm2_g, w_in, conv_w, a_log, dt_bias,
              o_norm_g, sgu_norm_g, w_s, b_s, w_out, ffn_w1, ffn_w3, ffn_w2, router_w, router_b,
              exp_w1, exp_w3, exp_w2, final_g):
    bsz, seq, dim = x.shape
    rows = seq // GRID_W
    x = x + grid_pos_embed(rows, dim).astype(x.dtype)[None]
    s_c = jax.nn.silu(c)
    s_ctx = jax.nn.silu(c_ctx)
    zero_state = jnp.zeros((bsz, GDN_HEADS, GDN_HEAD_DIM, GDN_HEAD_DIM), F32)

    def channel_mixer(layer, h):
        if layer % 2 == 0:
            i = layer // 2
            return swiglu(h, ffn_w1[i], ffn_w3[i], ffn_w2[i])
        i = layer // 2
        flat = moe_swiglu(h.reshape(-1, dim), router_w[i], router_b[i], exp_w1[i], exp_w3[i], exp_w2[i])
        return flat.reshape(h.shape)

    for layer in range(DEPTH):
        last = layer == DEPTH - 1
        mod = s_c @ w_mod[layer] + b_mod[layer]
        mod_c = s_ctx @ w_mod[layer] + b_mod[layer]
        sh1, sc1, g1, sh2, sc2, g2 = jnp.split(mod[:, None, :], 6, axis=-1)
        csh1, csc1, cg1, csh2, csc2, cg2 = jnp.split(mod_c, 6, axis=-1)

        pc = modulate(rms_norm(ctx, norm1_g[layer]), csh1, csc1) @ w_in[layer]
        oc, s_f, s_b = gdn_branch(pc, conv_w[layer], a_log[layer], dt_bias[layer], o_norm_g[layer],
                                  zero_state, zero_state)
        px = modulate(rms_norm(x, norm1_g[layer]), sh1, sc1) @ w_in[layer]
        ox, _, _ = gdn_branch(px, conv_w[layer], a_log[layer], dt_bias[layer], o_norm_g[layer], s_f, s_b)
        mx = gmlp_branch(px, sgu_norm_g[layer], w_s[layer], b_s[layer])
        x = x + g1 * (jnp.concatenate([ox, mx], axis=-1) @ w_out[layer])
        if not last:
            mc = gmlp_branch(pc, sgu_norm_g[layer], w_s[layer], b_s[layer])
            ctx = ctx + cg1 * (jnp.concatenate([oc, mc], axis=-1) @ w_out[layer])

        hx = modulate(rms_norm(x, norm2_g[layer]), sh2, sc2)
        x = x + g2 * channel_mixer(layer, hx)
        if not last:
            hc = modulate(rms_norm(ctx, norm2_g[layer]), csh2, csc2)
            ctx = ctx + cg2 * channel_mixer(layer, hc)

    return rms_norm(x, final_g)
```

```python
import functools

import jax
import jax.numpy as jnp
from jax import lax
from jax.experimental import pallas as pl
from jax.experimental.pallas import tpu as pltpu

F32 = jnp.float32
BF16 = jnp.bfloat16
EPS = 1e-6
HIGHEST = lax.Precision.HIGHEST

HEAD_DIM = 128
GDN_HEADS = 4
GDN_WIDTH = GDN_HEADS * HEAD_DIM
GDN_CHUNK = 64
CONV_K = 5
GMLP_GROUPS = 4
GMLP_GROUP_DIM = 128
GMLP_WIDTH = GMLP_GROUPS * GMLP_GROUP_DIM
GMLP_CHUNK = 128
GRID_W = 64
N_EXPERTS = 8

LANES = 128
SUBLANES = 8
VMEM_LIMIT = 56 * 1024 * 1024

TM = 256
HALO = SUBLANES
TF = 512
MOE_ROWS = 512
FC = 512
GATE_PAD = LANES


def _sigmoid(x):
    return 1.0 / (1.0 + jnp.exp(-x))


def _silu(x):
    return x * _sigmoid(x)


def _gelu_tanh(x):
    c = 0.7978845608028654
    return 0.5 * x * (1.0 + jnp.tanh(c * (x + 0.044715 * (x * x * x))))


def _softplus(x):
    return jnp.maximum(x, 0.0) + jnp.log1p(jnp.exp(-jnp.abs(x)))


def _rms_mod(x, g, shift, scale):
    ms = jnp.mean(x * x, axis=-1, keepdims=True)
    return (x * lax.rsqrt(ms + EPS) * g) * (1.0 + scale) + shift


def _params(n_axes=1):
    return pltpu.CompilerParams(dimension_semantics=("arbitrary",) * n_axes,
                                vmem_limit_bytes=VMEM_LIMIT)


def _mod_kernel(c_ref, w_ref, b_ref, o_ref):
    s = _silu(c_ref[...])
    o_ref[0] = jnp.dot(s, w_ref[0], precision=HIGHEST, preferred_element_type=F32) + b_ref[0]


def _modulation(cond, w_mod, b_mod):
    depth, d, six_d = w_mod.shape
    rows = cond.shape[0]
    tn = 1536
    return pl.pallas_call(
        _mod_kernel,
        grid=(depth, six_d // tn),
        in_specs=[pl.BlockSpec((rows, d), lambda l, n: (0, 0)),
                  pl.BlockSpec((1, d, tn), lambda l, n: (l, 0, n)),
                  pl.BlockSpec((1, 1, tn), lambda l, n: (l, 0, n))],
        out_specs=pl.BlockSpec((1, rows, tn), lambda l, n: (l, 0, n)),
        out_shape=jax.ShapeDtypeStruct((depth, rows, six_d), F32),
        compiler_params=_params(2),
        name="modulation",
    )(cond, w_mod, b_mod.reshape(depth, 1, six_d))


def _in_stage_kernel(add_pos, nx_tiles, tps, ctps, *refs):
    if add_pos:
        (x_ref, xp_ref, xn_ref, pos_ref, posp_ref, posn_ref, g_ref, sh_ref, sc_ref, w_ref, cw_ref,
         alog_ref, dt_ref, sg_ref, ws_ref, bs_ref, qkv_ref, z_ref, gate_ref, mx_ref, pe_scr) = refs
    else:
        (x_ref, xp_ref, xn_ref, g_ref, sh_ref, sc_ref, w_ref, cw_ref,
         alog_ref, dt_ref, sg_ref, ws_ref, bs_ref, qkv_ref, z_ref, gate_ref, mx_ref, pe_scr) = refs
    j = pl.program_id(0)
    is_x = j < nx_tiles
    tis = jnp.where(is_x, j % tps, (j - nx_tiles) % ctps)
    n_in_seq = jnp.where(is_x, tps, ctps)
    first = tis == 0
    last = tis == n_in_seq - 1

    xm, xp, xn = x_ref[...], xp_ref[...], xn_ref[...]
    if add_pos:
        xm, xp, xn = xm + pos_ref[...], xp + posp_ref[...], xn + posn_ref[...]
    xe = jnp.concatenate([xp, xm, xn], axis=0)
    h = _rms_mod(xe, g_ref[...], sh_ref[0], sc_ref[0])
    p = jnp.dot(h.astype(BF16), w_ref[...], preferred_element_type=F32)

    qkv_w = 3 * GDN_WIDTH
    rows = lax.broadcasted_iota(jnp.int32, (TM + 2 * HALO, 1), 0)
    valid = jnp.logical_and(jnp.logical_or(rows >= HALO, jnp.logical_not(first)),
                            jnp.logical_or(rows < TM + HALO, jnp.logical_not(last)))
    pe_scr[...] = jnp.where(valid, p[:, :qkv_w], 0.0)
    acc = cw_ref[0:1, :] * pe_scr[pl.ds(HALO - CONV_K // 2, TM), :]
    for t in range(1, CONV_K):
        acc = acc + cw_ref[t:t + 1, :] * pe_scr[pl.ds(HALO - CONV_K // 2 + t, TM), :]
    qkv = _silu(acc)
    for i in range(3 * GDN_HEADS):
        blk = qkv[:, i * HEAD_DIM:(i + 1) * HEAD_DIM]
        if i < 2 * GDN_HEADS:
            blk = blk * lax.rsqrt(jnp.sum(blk * blk, axis=-1, keepdims=True) + EPS)
        if i < GDN_HEADS:
            blk = blk * (HEAD_DIM ** -0.5)
        qkv_ref[:, i * HEAD_DIM:(i + 1) * HEAD_DIM] = blk

    pm = p[HALO:HALO + TM, :]
    z_ref[...] = _silu(pm[:, qkv_w:qkv_w + GDN_WIDTH])

    c0 = qkv_w + GDN_WIDTH
    gu = _gelu_tanh(pm[:, c0:c0 + GMLP_WIDTH])
    gvr = _gelu_tanh(pm[:, c0 + GMLP_WIDTH:c0 + 2 * GMLP_WIDTH])
    mu = jnp.mean(gvr, axis=-1, keepdims=True)
    xc = gvr - mu
    gv = xc * lax.rsqrt(jnp.mean(xc * xc, axis=-1, keepdims=True) + EPS) * sg_ref[...]
    for r in range(TM // GMLP_CHUNK):
        rs = slice(r * GMLP_CHUNK, (r + 1) * GMLP_CHUNK)
        for g in range(GMLP_GROUPS):
            cs = slice(g * GMLP_GROUP_DIM, (g + 1) * GMLP_GROUP_DIM)
            mixed = jnp.dot(ws_ref[g], gv[rs, cs].astype(BF16), preferred_element_type=F32) + bs_ref[g]
            mx_ref[rs, cs] = gu[rs, cs] * mixed

    graw = pm[:, c0 + 2 * GMLP_WIDTH:]
    lane = lax.broadcasted_iota(jnp.int32, graw.shape, 1)
    is_decay = (lane % (2 * GDN_HEADS)) < GDN_HEADS
    decay = -jnp.exp(alog_ref[...]) * _softplus(graw + dt_ref[...])
    gate_ref[...] = jnp.where(is_decay, decay, _sigmoid(graw))


def _in_stage(xa, pos_ext, norm_g, sh, sc, w_in_r, conv_w, alog_vec, dt_vec, sgu_g, w_s, b_s_b,
              bsz, seq, ctx_len):
    t_all, d = xa.shape
    nx_tiles = bsz * seq // TM
    tps, ctps = seq // TM, ctx_len // TM
    n_tiles = t_all // TM
    hb = TM // HALO
    add_pos = pos_ext is not None
    pw = w_in_r.shape[1]

    def mod_idx(j):
        return (jnp.where(j < nx_tiles, j // tps, bsz), 0, 0)

    in_specs = [pl.BlockSpec((TM, d), lambda j: (j, 0)),
                pl.BlockSpec((HALO, d), lambda j: (jnp.maximum(j * hb - 1, 0), 0)),
                pl.BlockSpec((HALO, d), lambda j: (jnp.minimum((j + 1) * hb, t_all // HALO - 1), 0))]
    args = [xa, xa, xa]
    if add_pos:
        in_specs += [
            pl.BlockSpec((TM, d), lambda j: (jnp.where(j < nx_tiles, j % tps, tps), 0)),
            pl.BlockSpec((HALO, d), lambda j: (jnp.where(j < nx_tiles, jnp.maximum((j % tps) * hb - 1, 0),
                                                         seq // HALO), 0)),
            pl.BlockSpec((HALO, d), lambda j: (jnp.where(j < nx_tiles, (j % tps + 1) * hb, seq // HALO), 0))]
        args += [pos_ext, pos_ext, pos_ext]
    const2 = lambda j: (0, 0)
    const3 = lambda j: (0, 0, 0)
    in_specs += [pl.BlockSpec((1, d), const2),
                 pl.BlockSpec((1, 1, d), mod_idx), pl.BlockSpec((1, 1, d), mod_idx),
                 pl.BlockSpec((d, pw), const2),
                 pl.BlockSpec((CONV_K, 3 * GDN_WIDTH), const2),
                 pl.BlockSpec((1, GATE_PAD), const2), pl.BlockSpec((1, GATE_PAD), const2),
                 pl.BlockSpec((1, GMLP_WIDTH), const2),
                 pl.BlockSpec((GMLP_GROUPS, GMLP_CHUNK, GMLP_CHUNK), const3),
                 pl.BlockSpec((GMLP_GROUPS, GMLP_CHUNK, GMLP_GROUP_DIM), const3)]
    args += [norm_g, sh, sc, w_in_r, conv_w, alog_vec, dt_vec, sgu_g, w_s, b_s_b]
    widths = (3 * GDN_WIDTH, GDN_WIDTH, GATE_PAD, GMLP_WIDTH)
    return pl.pallas_call(
        functools.partial(_in_stage_kernel, add_pos, nx_tiles, tps, ctps),
        grid=(n_tiles,),
        in_specs=in_specs,
        out_specs=[pl.BlockSpec((TM, w), lambda j: (j, 0)) for w in widths],
        out_shape=[jax.ShapeDtypeStruct((t_all, w), F32) for w in widths],
        scratch_shapes=[pltpu.VMEM((TM + 2 * HALO, 3 * GDN_WIDTH), F32)],
        compiler_params=_params(1),
        name="in_stage",
    )(*args)


INV_BASE = 16


def _dot_f32(a, b):
    return jnp.dot(a, b, precision=HIGHEST, preferred_element_type=F32)


def _unit_triangular_inverse(a, eye_f, bi, bj):
    c = a.shape[0]
    x = jnp.where(bi == bj, -a, 0.0)
    inv = eye_f + x
    pw = _dot_f32(x, x)
    n_sq = INV_BASE.bit_length() - 1
    for step in range(1, n_sq):
        if step < n_sq - 1:
            r = _dot_f32(jnp.concatenate([inv, pw], axis=0), pw)
            inv, pw = inv + r[:c], r[c:]
        else:
            inv = inv + _dot_f32(inv, pw)
    size = INV_BASE
    while size < c:
        grp = 2 * size // INV_BASE
        off = jnp.logical_and(bi // grp == bj // grp, bi // (grp // 2) != bj // (grp // 2))
        inv = inv - _dot_f32(_dot_f32(inv, jnp.where(off, a, 0.0)), inv)
        size *= 2
    return inv


def _gdn_direction(qkv_ref, gate_ref, s_ref, o_ref, reverse):
    c = GDN_CHUNK
    ii = lax.broadcasted_iota(jnp.int32, (c, c), 0)
    jj = lax.broadcasted_iota(jnp.int32, (c, c), 1)
    if reverse:
        incl, strict, lane0 = jj >= ii, jj > ii, 2 * GDN_HEADS
    else:
        incl, strict, lane0 = jj <= ii, jj < ii, 0
    eye = ii == jj
    eye_f = eye.astype(F32)
    bi, bj = ii // INV_BASE, jj // INV_BASE
    gates = gate_ref[...]
    gc_all = jnp.dot(incl.astype(F32), gates, precision=HIGHEST, preferred_element_type=F32)
    tot_all = gc_all[0:1, :] if reverse else gc_all[c - 1:c, :]
    contract_last = (((1,), (1,)), ((), ()))
    contract_first = (((0,), (0,)), ((), ()))
    for h in range(GDN_HEADS):
        q = qkv_ref[:, h * HEAD_DIM:(h + 1) * HEAD_DIM]
        k = qkv_ref[:, GDN_WIDTH + h * HEAD_DIM:GDN_WIDTH + (h + 1) * HEAD_DIM]
        v = qkv_ref[:, 2 * GDN_WIDTH + h * HEAD_DIM:2 * GDN_WIDTH + (h + 1) * HEAD_DIM]
        gc = gc_all[:, lane0 + h:lane0 + h + 1]
        beta = gates[:, lane0 + GDN_HEADS + h:lane0 + GDN_HEADS + h + 1]
        tot = tot_all[:, lane0 + h:lane0 + h + 1]
        gcb = jnp.broadcast_to(gc, (c, c))
        gc_row = jnp.sum(jnp.where(eye, gcb, 0.0), axis=0, keepdims=True)
        gamma = jnp.where(incl, jnp.exp(jnp.where(incl, gcb - gc_row, 0.0)), 0.0)
        egc = jnp.exp(gc)
        kb = k * beta
        kq = lax.dot_general(jnp.concatenate([kb, q], axis=0).astype(BF16), k.astype(BF16),
                             contract_last, preferred_element_type=F32)
        a = jnp.where(strict, kq[:c] * gamma, 0.0)
        qk = jnp.where(incl, kq[c:] * gamma, 0.0)
        tm = _unit_triangular_inverse(a, eye_f, bi, bj)
        uw = jnp.dot(tm.astype(BF16), jnp.concatenate([v * beta, kb * egc], axis=1).astype(BF16),
                     preferred_element_type=F32)
        u, w = uw[:, :HEAD_DIM], uw[:, HEAD_DIM:]
        s = s_ref[h]
        wq = jnp.dot(jnp.concatenate([w, q * egc], axis=0).astype(BF16), s.astype(BF16),
                     preferred_element_type=F32)
        v_new = u - wq[:c]
        vb = v_new.astype(BF16)
        o_ref[:, h * HEAD_DIM:(h + 1) * HEAD_DIM] = wq[c:] + jnp.dot(qk.astype(BF16), vb,
                                                                    preferred_element_type=F32)
        k_dec = (k * jnp.exp(tot - gc)).astype(BF16)
        s_ref[h] = s * jnp.exp(tot) + lax.dot_general(k_dec, vb, contract_first, preferred_element_type=F32)


def _gdn_kernel(qf_ref, gf_ref, qb_ref, gb_ref, of_ref, ob_ref, sf_ref, sb_ref):
    @pl.when(pl.program_id(1) == 0)
    def _():
        sf_ref[...] = jnp.zeros_like(sf_ref)
        sb_ref[...] = jnp.zeros_like(sb_ref)

    _gdn_direction(qf_ref, gf_ref, sf_ref, of_ref, reverse=False)
    _gdn_direction(qb_ref, gb_ref, sb_ref, ob_ref, reverse=True)


def _gdn_scan(qkv, gates, bsz, seq, ctx_len):
    t_all = qkv.shape[0]
    c = GDN_CHUNK
    n_ctx, n_x = ctx_len // c, seq // c
    n_steps = n_ctx + n_x
    ctx_base = bsz * n_x

    def fwd_idx(b, s):
        return (jnp.where(s < n_ctx, ctx_base + b * n_ctx + s, b * n_x + s - n_ctx), 0)

    def bwd_idx(b, s):
        return (jnp.where(s < n_ctx, ctx_base + b * n_ctx + n_ctx - 1 - s, b * n_x + n_steps - 1 - s), 0)

    qw = 3 * GDN_WIDTH
    return pl.pallas_call(
        _gdn_kernel,
        grid=(bsz, n_steps),
        in_specs=[pl.BlockSpec((c, qw), fwd_idx), pl.BlockSpec((c, GATE_PAD), fwd_idx),
                  pl.BlockSpec((c, qw), bwd_idx), pl.BlockSpec((c, GATE_PAD), bwd_idx)],
        out_specs=[pl.BlockSpec((c, GDN_WIDTH), fwd_idx), pl.BlockSpec((c, GDN_WIDTH), bwd_idx)],
        out_shape=[jax.ShapeDtypeStruct((t_all, GDN_WIDTH), F32)] * 2,
        scratch_shapes=[pltpu.VMEM((GDN_HEADS, HEAD_DIM, HEAD_DIM), F32)] * 2,
        compiler_params=_params(2),
        name="gdn_scan",
    )(qkv, gates, qkv, gates)


def _out_stage_kernel(add_pos, *refs):
    if add_pos:
        x_ref, pos_ref, of_ref, ob_ref, z_ref, mx_ref, og_ref, g1_ref, w_ref, o_ref = refs
    else:
        x_ref, of_ref, ob_ref, z_ref, mx_ref, og_ref, g1_ref, w_ref, o_ref = refs
    o = of_ref[...] + ob_ref[...]
    zs = z_ref[...]
    parts = []
    for h in range(GDN_HEADS):
        hs = slice(h * HEAD_DIM, (h + 1) * HEAD_DIM)
        oh = o[:, hs]
        oh = oh * lax.rsqrt(jnp.mean(oh * oh, axis=-1, keepdims=True) + EPS) * og_ref[...]
        parts.append((oh * zs[:, hs]).astype(BF16))
    parts.append(mx_ref[...].astype(BF16))
    y = jnp.dot(jnp.concatenate(parts, axis=-1), w_ref[...], preferred_element_type=F32)
    x = x_ref[...]
    if add_pos:
        x = x + pos_ref[...]
    o_ref[...] = x + g1_ref[0] * y


def _out_stage(xa, pos_ext, o_f, o_b, zs, mx, o_norm_g, g1, w_out, n_rows, bsz, seq):
    d = xa.shape[1]
    nx_tiles, tps = bsz * seq // TM, seq // TM
    add_pos = pos_ext is not None
    row = lambda w: pl.BlockSpec((TM, w), lambda j: (j, 0))
    in_specs, args = [row(d)], [xa]
    if add_pos:
        in_specs.append(pl.BlockSpec((TM, d), lambda j: (jnp.where(j < nx_tiles, j % tps, tps), 0)))
        args.append(pos_ext)
    in_specs += [row(GDN_WIDTH), row(GDN_WIDTH), row(GDN_WIDTH), row(GMLP_WIDTH),
                 pl.BlockSpec((1, HEAD_DIM), lambda j: (0, 0)),
                 pl.BlockSpec((1, 1, d), lambda j: (jnp.where(j < nx_tiles, j // tps, bsz), 0, 0)),
                 pl.BlockSpec((GDN_WIDTH + GMLP_WIDTH, d), lambda j: (0, 0))]
    args += [o_f, o_b, zs, mx, o_norm_g, g1, w_out]
    return pl.pallas_call(
        functools.partial(_out_stage_kernel, add_pos),
        grid=(n_rows // TM,),
        in_specs=in_specs,
        out_specs=row(d),
        out_shape=jax.ShapeDtypeStruct((n_rows, d), F32),
        compiler_params=_params(1),
        name="out_stage",
    )(*args)


def _swiglu_body(hb, w1, w3, w2):
    d_ff = w1.shape[1]
    acc = None
    for f0 in range(0, d_ff, FC):
        a = jnp.dot(hb, w1[:, f0:f0 + FC], preferred_element_type=F32)
        b = jnp.dot(hb, w3[:, f0:f0 + FC], preferred_element_type=F32)
        part = jnp.dot((_silu(a) * b).astype(BF16), w2[f0:f0 + FC, :], preferred_element_type=F32)
        acc = part if acc is None else acc + part
    return acc


def _ffn_kernel(x_ref, g_ref, sh_ref, sc_ref, g2_ref, w1_ref, w3_ref, w2_ref, o_ref):
    x = x_ref[...]
    h = _rms_mod(x, g_ref[...], sh_ref[0], sc_ref[0])
    o_ref[...] = x + g2_ref[0] * _swiglu_body(h.astype(BF16), w1_ref, w3_ref, w2_ref)


def _dense_ffn(xa, norm_g, sh, sc, g2, w1, w3, w2, bsz, seq):
    t_all, d = xa.shape
    d_ff = w1.shape[1]
    nx_tiles, tps = bsz * seq // TF, seq // TF
    mod_idx = lambda j: (jnp.where(j < nx_tiles, j // tps, bsz), 0, 0)
    resident = lambda shape: pl.BlockSpec(shape, lambda j: (0, 0), pipeline_mode=pl.Buffered(1))
    return pl.pallas_call(
        _ffn_kernel,
        grid=(t_all // TF,),
        in_specs=[pl.BlockSpec((TF, d), lambda j: (j, 0)),
                  pl.BlockSpec((1, d), lambda j: (0, 0)),
                  pl.BlockSpec((1, 1, d), mod_idx), pl.BlockSpec((1, 1, d), mod_idx),
                  pl.BlockSpec((1, 1, d), mod_idx),
                  resident((d, d_ff)), resident((d, d_ff)), resident((d_ff, d))],
        out_specs=pl.BlockSpec((TF, d), lambda j: (j, 0)),
        out_shape=jax.ShapeDtypeStruct((t_all, d), F32),
        compiler_params=_params(1),
        name="dense_ffn",
    )(xa, norm_g, sh, sc, g2, w1, w3, w2)


def _router_kernel(x_ref, g_ref, sh_ref, sc_ref, rw_ref, rb_ref, h_ref, route_ref, cnt_ref, run_scr):
    i = pl.program_id(0)

    @pl.when(i == 0)
    def _():
        run_scr[...] = jnp.zeros_like(run_scr)

    h = _rms_mod(x_ref[...], g_ref[...], sh_ref[0], sc_ref[0])
    h_ref[...] = h
    logits = jnp.dot(h, rw_ref[...], precision=HIGHEST, preferred_element_type=F32) + rb_ref[...]
    lane = lax.broadcasted_iota(jnp.int32, logits.shape, 1)
    m1 = jnp.max(logits, axis=-1, keepdims=True)
    i1 = jnp.min(jnp.where(logits == m1, lane, LANES), axis=-1, keepdims=True)
    rest = jnp.where(lane == i1, -jnp.inf, logits)
    m2 = jnp.max(rest, axis=-1, keepdims=True)
    i2 = jnp.min(jnp.where(rest == m2, lane, LANES), axis=-1, keepdims=True)
    e21 = jnp.exp(m2 - m1)
    p1 = 1.0 / (1.0 + e21)
    p2 = e21 * p1
    oh1 = (lane == i1).astype(F32)
    oh2 = (lane == i2).astype(F32)
    oh = oh1 + oh2
    rows = h.shape[0]
    ri = lax.broadcasted_iota(jnp.int32, (rows, rows), 0)
    ci = lax.broadcasted_iota(jnp.int32, (rows, rows), 1)
    earlier = (ci < ri).astype(BF16)
    before = jnp.dot(earlier, oh.astype(BF16), preferred_element_type=F32) + run_scr[...]
    r1 = jnp.sum(oh1 * before, axis=-1, keepdims=True)
    r2 = jnp.sum(oh2 * before, axis=-1, keepdims=True)
    run = run_scr[...] + jnp.sum(oh, axis=0, keepdims=True)
    run_scr[...] = run
    cnt_ref[...] = run
    vals = (p1, p2, i1.astype(F32), i2.astype(F32), r1, r2)
    out = jnp.zeros_like(logits)
    for n, val in enumerate(vals):
        out = jnp.where(lane == n, val, out)
    route_ref[...] = out


def _router(x1, norm_g, sh, sc, rw_pad, rb_pad, seq):
    n, d = x1.shape
    tps = seq // TF
    mod_idx = lambda j: (j // tps, 0, 0)
    return pl.pallas_call(
        _router_kernel,
        grid=(n // TF,),
        in_specs=[pl.BlockSpec((TF, d), lambda j: (j, 0)),
                  pl.BlockSpec((1, d), lambda j: (0, 0)),
                  pl.BlockSpec((1, 1, d), mod_idx), pl.BlockSpec((1, 1, d), mod_idx),
                  pl.BlockSpec((d, LANES), lambda j: (0, 0)),
                  pl.BlockSpec((1, LANES), lambda j: (0, 0))],
        out_specs=[pl.BlockSpec((TF, d), lambda j: (j, 0)),
                   pl.BlockSpec((TF, LANES), lambda j: (j, 0)),
                   pl.BlockSpec((1, LANES), lambda j: (0, 0))],
        out_shape=[jax.ShapeDtypeStruct((n, d), F32),
                   jax.ShapeDtypeStruct((n, LANES), F32),
                   jax.ShapeDtypeStruct((1, LANES), F32)],
        scratch_shapes=[pltpu.VMEM((1, LANES), F32)],
        compiler_params=_params(1),
        name="router",
    )(x1, norm_g, sh, sc, rw_pad, rb_pad)


def _dispatch_kernel(dest_ref, h_ref, xb_in_ref, xb_ref, sem):
    del xb_in_ref
    rows = h_ref.shape[0]

    def issue(t, carry):
        for kk in range(2):
            pltpu.make_async_copy(h_ref.at[pl.ds(t, 1)], xb_ref.at[pl.ds(dest_ref[0, 0, kk * rows + t], 1)],
                                  sem).start()
        return carry

    lax.fori_loop(0, rows, issue, 0)
    for kk in range(2):
        pltpu.make_async_copy(h_ref, xb_ref.at[pl.ds(0, rows)], sem).wait()


def _dispatch(hx, dest_tiles, xb_init):
    n, d = hx.shape
    return pl.pallas_call(
        _dispatch_kernel,
        grid=(n // TF,),
        in_specs=[pl.BlockSpec((1, 1, 2 * TF), lambda j: (j, 0, 0), memory_space=pltpu.SMEM),
                  pl.BlockSpec((TF, d), lambda j: (j, 0)),
                  pl.BlockSpec(memory_space=pl.ANY)],
        out_specs=pl.BlockSpec(memory_space=pl.ANY),
        out_shape=jax.ShapeDtypeStruct(xb_init.shape, F32),
        scratch_shapes=[pltpu.SemaphoreType.DMA(())],
        input_output_aliases={2: 0},
        compiler_params=_params(1),
        name="dispatch",
    )(dest_tiles, hx, xb_init)


def _grouped_kernel(be_ref, nb_ref, x_ref, w1_ref, w3_ref, w2_ref, o_ref):
    del be_ref

    @pl.when(pl.program_id(0) < nb_ref[0])
    def _():
        o_ref[...] = _swiglu_body(x_ref[...].astype(BF16), w1_ref.at[0], w3_ref.at[0], w2_ref.at[0])


def _grouped_swiglu(xb, block_e, nb_used, w1, w3, w2):
    r, d = xb.shape
    d_ff = w1.shape[2]
    row_idx = lambda i, be, nb: (jnp.minimum(i, nb[0] - 1), 0)
    w_idx = lambda i, be, nb: (be[i], 0, 0)
    expert = lambda shape: pl.BlockSpec(shape, w_idx, pipeline_mode=pl.Buffered(1))
    return pl.pallas_call(
        _grouped_kernel,
        grid_spec=pltpu.PrefetchScalarGridSpec(
            num_scalar_prefetch=2,
            grid=(r // MOE_ROWS,),
            in_specs=[pl.BlockSpec((MOE_ROWS, d), row_idx),
                      expert((1, d, d_ff)), expert((1, d, d_ff)), expert((1, d_ff, d))],
            out_specs=pl.BlockSpec((MOE_ROWS, d), row_idx)),
        out_shape=jax.ShapeDtypeStruct((r, d), F32),
        input_output_aliases={2: 0},
        compiler_params=_params(1),
        name="grouped_swiglu",
    )(block_e, nb_used, xb, w1, w3, w2)


def _combine_kernel(dest_ref, x_ref, route_ref, g2_ref, fg_ref, yb_ref, o_ref, buf, sem):
    rows = x_ref.shape[0]

    def issue(t, carry):
        for kk in range(2):
            pltpu.make_async_copy(yb_ref.at[pl.ds(dest_ref[0, 0, kk * rows + t], 1)],
                                  buf.at[kk, pl.ds(t, 1)], sem).start()
        return carry

    lax.fori_loop(0, rows, issue, 0)
    for kk in range(2):
        pltpu.make_async_copy(yb_ref.at[pl.ds(0, rows)], buf.at[kk], sem).wait()
    route = route_ref[...]
    y = route[:, 0:1] * buf[0] + route[:, 1:2] * buf[1]
    xn = x_ref[...] + g2_ref[0] * y
    o_ref[...] = xn * lax.rsqrt(jnp.mean(xn * xn, axis=-1, keepdims=True) + EPS) * fg_ref[...]


def _combine(x1, route, dest_tiles, g2, final_g, yb, seq):
    n, d = x1.shape
    tps = seq // TM
    return pl.pallas_call(
        _combine_kernel,
        grid=(n // TM,),
        in_specs=[pl.BlockSpec((1, 1, 2 * TM), lambda j: (j, 0, 0), memory_space=pltpu.SMEM),
                  pl.BlockSpec((TM, d), lambda j: (j, 0)),
                  pl.BlockSpec((TM, LANES), lambda j: (j, 0)),
                  pl.BlockSpec((1, 1, d), lambda j: (j // tps, 0, 0)),
                  pl.BlockSpec((1, d), lambda j: (0, 0)),
                  pl.BlockSpec(memory_space=pl.ANY)],
        out_specs=pl.BlockSpec((TM, d), lambda j: (j, 0)),
        out_shape=jax.ShapeDtypeStruct((n, d), F32),
        scratch_shapes=[pltpu.VMEM((2, TM, d), F32), pltpu.SemaphoreType.DMA(())],
        compiler_params=_params(1),
        name="combine",
    )(dest_tiles, x1, route, g2, final_g, yb)


def _moe_layer(x1, norm_g, sh, sc, g2, router_w, router_b, w1, w3, w2, final_g, seq):
    n, d = x1.shape
    rw_pad = jnp.zeros((d, LANES), F32).at[:, :N_EXPERTS].set(router_w)
    rb_pad = jnp.full((1, LANES), -1e30, F32).at[0, :N_EXPERTS].set(router_b)
    hx, route, counts = _router(x1, norm_g, sh, sc, rw_pad, rb_pad, seq)

    cnt = counts[0, :N_EXPERTS].astype(jnp.int32)
    padded = (cnt + MOE_ROWS - 1) // MOE_ROWS * MOE_ROWS
    seg_end = jnp.cumsum(padded)
    pad_start = seg_end - padded
    n_blocks = (2 * n) // MOE_ROWS + N_EXPERTS
    e12 = route[:, 2:4].astype(jnp.int32)
    r12 = route[:, 4:6].astype(jnp.int32)
    dest = pad_start[e12] + r12
    nb_used = (seg_end[-1] // MOE_ROWS).astype(jnp.int32).reshape(1)
    blk = jnp.minimum(jnp.arange(n_blocks, dtype=jnp.int32), nb_used[0] - 1) * MOE_ROWS
    block_e = jnp.minimum(jnp.searchsorted(seg_end, blk, side="right"), N_EXPERTS - 1).astype(jnp.int32)

    def tiles(rows):
        return dest.reshape(n // rows, rows, 2).transpose(0, 2, 1).reshape(n // rows, 1, 2 * rows)

    xb = _dispatch(hx, tiles(TF), jnp.zeros((n_blocks * MOE_ROWS, d), F32))
    yb = _grouped_swiglu(xb, block_e, nb_used, w1, w3, w2)
    return _combine(x1, route, tiles(TM), g2, final_g, yb, seq)


def _grid_pos_embed(rows, dim):
    row = jnp.broadcast_to(jnp.arange(rows, dtype=F32)[:, None], (rows, GRID_W)).reshape(-1)
    col = jnp.broadcast_to(jnp.arange(GRID_W, dtype=F32)[None, :], (rows, GRID_W)).reshape(-1)
    quarter = dim // 4
    omega = 1.0 / (10000.0 ** (jnp.arange(quarter, dtype=F32) / quarter))

    def enc(p):
        ang = p[:, None] * omega[None, :]
        return jnp.concatenate([jnp.sin(ang), jnp.cos(ang)], axis=-1)

    return jnp.concatenate([enc(row), enc(col)], axis=-1)


def _gate_lane_vec(t):
    v = jnp.zeros((2, 2 * GDN_HEADS), F32).at[:, :GDN_HEADS].set(t.astype(F32)).reshape(1, -1)
    return jnp.pad(v, ((0, 0), (0, GATE_PAD - v.shape[1])))


def kernel(x, c, ctx, c_ctx, w_mod, b_mod, norm1_g, norm2_g, w_in, conv_w, a_log, dt_bias, o_norm_g, sgu_norm_g, w_s, b_s, w_out, ffn_w1, ffn_w3, ffn_w2, router_w, router_b, exp_w1, exp_w3, exp_w2, final_g):
    bsz, seq, d = x.shape
    ctx_len = ctx.shape[1]
    depth = w_mod.shape[0]
    assert seq % TF == 0 and ctx_len % TM == 0 and (bsz * ctx_len) % TF == 0 and depth == 2
    n_lat = bsz * seq

    cond_rows = -(-(bsz + 1) // SUBLANES) * SUBLANES
    cond = jnp.zeros((cond_rows, d), F32).at[:bsz].set(c).at[bsz].set(c_ctx)
    mod = _modulation(cond, w_mod, b_mod)[:, :bsz + 1].reshape(depth, bsz + 1, 6, 1, d)

    pos_ext = jnp.concatenate([_grid_pos_embed(seq // GRID_W, d), jnp.zeros((TM, d), F32)], axis=0)
    xa = jnp.concatenate([x.reshape(n_lat, d), ctx.reshape(bsz * ctx_len, d)], axis=0)

    qkvz_w = 4 * GDN_WIDTH
    n_gate = 4 * GDN_HEADS
    out = None
    for layer in range(depth):
        last = layer == depth - 1
        sh1, sc1, g1, sh2, sc2, g2 = (mod[layer, :, i] for i in range(6))
        wl = w_in[layer]
        w_in_r = jnp.concatenate([wl[:, :qkvz_w], wl[:, qkvz_w + n_gate:], wl[:, qkvz_w:qkvz_w + n_gate],
                                  jnp.zeros((d, GATE_PAD - n_gate), F32)], axis=1).astype(BF16)
        b_s_b = jnp.broadcast_to(b_s[layer][:, :, None], (GMLP_GROUPS, GMLP_CHUNK, GMLP_GROUP_DIM))
        pos = pos_ext if layer == 0 else None
        qkv, zs, gates, mx = _in_stage(
            xa, pos, norm1_g[layer][None], sh1, sc1, w_in_r, conv_w[layer],
            _gate_lane_vec(a_log[layer]), _gate_lane_vec(dt_bias[layer]), sgu_norm_g[layer][None],
            w_s[layer].astype(BF16), b_s_b, bsz, seq, ctx_len)
        o_f, o_b = _gdn_scan(qkv, gates, bsz, seq, ctx_len)
        n_rows = n_lat if last else xa.shape[0]
        xa = _out_stage(xa, pos, o_f, o_b, zs, mx, o_norm_g[layer][None], g1, w_out[layer].astype(BF16),
                        n_rows, bsz, seq)
        i = layer // 2
        if layer % 2 == 0:
            xa = _dense_ffn(xa, norm2_g[layer][None], sh2, sc2, g2, ffn_w1[i].astype(BF16),
                            ffn_w3[i].astype(BF16), ffn_w2[i].astype(BF16), bsz, seq)
        else:
            out = _moe_layer(xa, norm2_g[layer][None], sh2, sc2, g2, router_w[i], router_b[i],
                             exp_w1[i].astype(BF16), exp_w3[i].astype(BF16), exp_w2[i].astype(BF16),
                             final_g[None], seq)
    return out.reshape(bsz, seq, d)
```

```python
import functools

import jax
import jax.numpy as jnp
from jax import lax
from jax.experimental import pallas as pl
from jax.experimental.pallas import tpu as pltpu

F32 = jnp.float32
BF16 = jnp.bfloat16
EPS = 1e-6
HIGHEST = lax.Precision.HIGHEST

HEAD_DIM = 128
GDN_HEADS = 4
GDN_WIDTH = GDN_HEADS * HEAD_DIM
GDN_CHUNK = 64
CONV_K = 5
GMLP_GROUPS = 4
GMLP_GROUP_DIM = 128
GMLP_WIDTH = GMLP_GROUPS * GMLP_GROUP_DIM
GMLP_CHUNK = 128
GRID_W = 64
N_EXPERTS = 8

LANES = 128
SUBLANES = 8
VMEM_LIMIT = 56 * 1024 * 1024

TM = 256
HALO = SUBLANES
TF = 512
MOE_ROWS = 512
FC = 512
GATE_PAD = LANES


def _sigmoid(x):
    return 1.0 / (1.0 + jnp.exp(-x))


def _silu(x):
    return x * _sigmoid(x)


def _gelu_tanh(x):
    c = 0.7978845608028654
    return 0.5 * x * (1.0 + jnp.tanh(c * (x + 0.044715 * (x * x * x))))


def _softplus(x):
    return jnp.maximum(x, 0.0) + jnp.log1p(jnp.exp(-jnp.abs(x)))


def _rms_mod(x, g, shift, scale):
    ms = jnp.mean(x * x, axis=-1, keepdims=True)
    return (x * lax.rsqrt(ms + EPS) * g) * (1.0 + scale) + shift


def _params(n_axes=1):
    return pltpu.CompilerParams(dimension_semantics=("arbitrary",) * n_axes,
                                vmem_limit_bytes=VMEM_LIMIT)


def _mod_kernel(c_ref, w_ref, b_ref, o_ref):
    s = _silu(c_ref[...])
    o_ref[0] = jnp.dot(s, w_ref[0], precision=HIGHEST, preferred_element_type=F32) + b_ref[0]


def _modulation(cond, w_mod, b_mod):
    depth, d, six_d = w_mod.shape
    rows = cond.shape[0]
    tn = 1536
    return pl.pallas_call(
        _mod_kernel,
        grid=(depth, six_d // tn),
        in_specs=[pl.BlockSpec((rows, d), lambda l, n: (0, 0)),
                  pl.BlockSpec((1, d, tn), lambda l, n: (l, 0, n)),
                  pl.BlockSpec((1, 1, tn), lambda l, n: (l, 0, n))],
        out_specs=pl.BlockSpec((1, rows, tn), lambda l, n: (l, 0, n)),
        out_shape=jax.ShapeDtypeStruct((depth, rows, six_d), F32),
        compiler_params=_params(2),
        name="modulation",
    )(cond, w_mod, b_mod.reshape(depth, 1, six_d))


def _in_stage_kernel(add_pos, nx_tiles, tps, ctps, *refs):
    if add_pos:
        (x_ref, xp_ref, xn_ref, pos_ref, posp_ref, posn_ref, g_ref, sh_ref, sc_ref, w_ref, cw_ref,
         alog_ref, dt_ref, sg_ref, ws_ref, bs_ref, qkv_ref, z_ref, gate_ref, mx_ref, pe_scr) = refs
    else:
        (x_ref, xp_ref, xn_ref, g_ref, sh_ref, sc_ref, w_ref, cw_ref,
         alog_ref, dt_ref, sg_ref, ws_ref, bs_ref, qkv_ref, z_ref, gate_ref, mx_ref, pe_scr) = refs
    j = pl.program_id(0)
    is_x = j < nx_tiles
    tis = jnp.where(is_x, j % tps, (j - nx_tiles) % ctps)
    n_in_seq = jnp.where(is_x, tps, ctps)
    first = tis == 0
    last = tis == n_in_seq - 1

    xm, xp, xn = x_ref[...], xp_ref[...], xn_ref[...]
    if add_pos:
        xm, xp, xn = xm + pos_ref[...], xp + posp_ref[...], xn + posn_ref[...]
    xe = jnp.concatenate([xp, xm, xn], axis=0)
    h = _rms_mod(xe, g_ref[...], sh_ref[0], sc_ref[0])
    p = jnp.dot(h.astype(BF16), w_ref[...], preferred_element_type=F32)

    qkv_w = 3 * GDN_WIDTH
    rows = lax.broadcasted_iota(jnp.int32, (TM + 2 * HALO, 1), 0)
    valid = jnp.logical_and(jnp.logical_or(rows >= HALO, jnp.logical_not(first)),
                            jnp.logical_or(rows < TM + HALO, jnp.logical_not(last)))
    pe_scr[...] = jnp.where(valid, p[:, :qkv_w], 0.0)
    acc = cw_ref[0:1, :] * pe_scr[pl.ds(HALO - CONV_K // 2, TM), :]
    for t in range(1, CONV_K):
        acc = acc + cw_ref[t:t + 1, :] * pe_scr[pl.ds(HALO - CONV_K // 2 + t, TM), :]
    qkv = _silu(acc)
    for i in range(3 * GDN_HEADS):
        blk = qkv[:, i * HEAD_DIM:(i + 1) * HEAD_DIM]
        if i < 2 * GDN_HEADS:
            blk = blk * lax.rsqrt(jnp.sum(blk * blk, axis=-1, keepdims=True) + EPS)
        if i < GDN_HEADS:
            blk = blk * (HEAD_DIM ** -0.5)
        qkv_ref[:, i * HEAD_DIM:(i + 1) * HEAD_DIM] = blk

    pm = p[HALO:HALO + TM, :]
    z_ref[...] = _silu(pm[:, qkv_w:qkv_w + GDN_WIDTH])

    c0 = qkv_w + GDN_WIDTH
    gu = _gelu_tanh(pm[:, c0:c0 + GMLP_WIDTH])
    gvr = _gelu_tanh(pm[:, c0 + GMLP_WIDTH:c0 + 2 * GMLP_WIDTH])
    mu = jnp.mean(gvr, axis=-1, keepdims=True)
    xc = gvr - mu
    gv = xc * lax.rsqrt(jnp.mean(xc * xc, axis=-1, keepdims=True) + EPS) * sg_ref[...]
    for r in range(TM // GMLP_CHUNK):
        rs = slice(r * GMLP_CHUNK, (r + 1) * GMLP_CHUNK)
        for g in range(GMLP_GROUPS):
            cs = slice(g * GMLP_GROUP_DIM, (g + 1) * GMLP_GROUP_DIM)
            mixed = jnp.dot(ws_ref[g], gv[rs, cs].astype(BF16), preferred_element_type=F32) + bs_ref[g]
            mx_ref[rs, cs] = gu[rs, cs] * mixed

    graw = pm[:, c0 + 2 * GMLP_WIDTH:]
    lane = lax.broadcasted_iota(jnp.int32, graw.shape, 1)
    is_decay = (lane % (2 * GDN_HEADS)) < GDN_HEADS
    decay = -jnp.exp(alog_ref[...]) * _softplus(graw + dt_ref[...])
    gate_ref[...] = jnp.where(is_decay, decay, _sigmoid(graw))


def _in_stage(xa, pos_ext, norm_g, sh, sc, w_in_r, conv_w, alog_vec, dt_vec, sgu_g, w_s, b_s_b,
              bsz, seq, ctx_len):
    t_all, d = xa.shape
    nx_tiles = bsz * seq // TM
    tps, ctps = seq // TM, ctx_len // TM
    n_tiles = t_all // TM
    hb = TM // HALO
    add_pos = pos_ext is not None
    pw = w_in_r.shape[1]

    def mod_idx(j):
        return (jnp.where(j < nx_tiles, j // tps, bsz), 0, 0)

    in_specs = [pl.BlockSpec((TM, d), lambda j: (j, 0)),
                pl.BlockSpec((HALO, d), lambda j: (jnp.maximum(j * hb - 1, 0), 0)),
                pl.BlockSpec((HALO, d), lambda j: (jnp.minimum((j + 1) * hb, t_all // HALO - 1), 0))]
    args = [xa, xa, xa]
    if add_pos:
        in_specs += [
            pl.BlockSpec((TM, d), lambda j: (jnp.where(j < nx_tiles, j % tps, tps), 0)),
            pl.BlockSpec((HALO, d), lambda j: (jnp.where(j < nx_tiles, jnp.maximum((j % tps) * hb - 1, 0),
                                                         seq // HALO), 0)),
            pl.BlockSpec((HALO, d), lambda j: (jnp.where(j < nx_tiles, (j % tps + 1) * hb, seq // HALO), 0))]
        args += [pos_ext, pos_ext, pos_ext]
    const2 = lambda j: (0, 0)
    const3 = lambda j: (0, 0, 0)
    in_specs += [pl.BlockSpec((1, d), const2),
                 pl.BlockSpec((1, 1, d), mod_idx), pl.BlockSpec((1, 1, d), mod_idx),
                 pl.BlockSpec((d, pw), const2),
                 pl.BlockSpec((CONV_K, 3 * GDN_WIDTH), const2),
                 pl.BlockSpec((1, GATE_PAD), const2), pl.BlockSpec((1, GATE_PAD), const2),
                 pl.BlockSpec((1, GMLP_WIDTH), const2),
                 pl.BlockSpec((GMLP_GROUPS, GMLP_CHUNK, GMLP_CHUNK), const3),
                 pl.BlockSpec((GMLP_GROUPS, GMLP_CHUNK, GMLP_GROUP_DIM), const3)]
    args += [norm_g, sh, sc, w_in_r, conv_w, alog_vec, dt_vec, sgu_g, w_s, b_s_b]
    widths = (3 * GDN_WIDTH, GDN_WIDTH, GATE_PAD, GMLP_WIDTH)
    return pl.pallas_call(
        functools.partial(_in_stage_kernel, add_pos, nx_tiles, tps, ctps),
        grid=(n_tiles,),
        in_specs=in_specs,
        out_specs=[pl.BlockSpec((TM, w), lambda j: (j, 0)) for w in widths],
        out_shape=[jax.ShapeDtypeStruct((t_all, w), F32) for w in widths],
        scratch_shapes=[pltpu.VMEM((TM + 2 * HALO, 3 * GDN_WIDTH), F32)],
        compiler_params=_params(1),
        name="in_stage",
    )(*args)


INV_BASE = 16


GDN_BATCHES = 4


def _dot_bf16(a, b, dims=None):
    a, b = a.astype(BF16), b.astype(BF16)
    if dims is None:
        return jnp.dot(a, b, preferred_element_type=F32)
    return lax.dot_general(a, b, (dims, ((), ())), preferred_element_type=F32)


def _dot_3x(a, b):
    ah = a.astype(BF16)
    al = (a - ah.astype(F32)).astype(BF16)
    bh = b.astype(BF16)
    bl = (b - bh.astype(F32)).astype(BF16)
    dot = lambda p, q: jnp.dot(p, q, preferred_element_type=F32)
    return dot(ah, bh) + (dot(ah, bl) + dot(al, bh))


def _gdn_kernel(n_b, *refs):
    ins, outs, s_ref = refs[:4 * n_b], refs[4 * n_b:4 * n_b + 2], refs[4 * n_b + 2]

    @pl.when(pl.program_id(1) == 0)
    def _():
        s_ref[...] = jnp.zeros_like(s_ref)

    c = GDN_CHUNK
    ii = lax.broadcasted_iota(jnp.int32, (c, c), 0)
    jj = lax.broadcasted_iota(jnp.int32, (c, c), 1)
    eye = ii == jj
    eye_f = eye.astype(F32)
    bi, bj = ii // INV_BASE, jj // INV_BASE
    diag_blk = bi == bj
    merge_masks = []
    size = INV_BASE
    while size < c:
        grp = 2 * size // INV_BASE
        merge_masks.append(jnp.logical_and(bi // grp == bj // grp, bi // (grp // 2) != bj // (grp // 2)))
        size *= 2

    chains = []
    for g in range(n_b):
        for reverse in (False, True):
            qkv_ref, gate_ref = ins[4 * g + 2 * reverse], ins[4 * g + 2 * reverse + 1]
            incl, strict = (jj >= ii, jj > ii) if reverse else (jj <= ii, jj < ii)
            lane0 = 2 * GDN_HEADS * reverse
            gates = gate_ref[...]
            gc_all = jnp.dot(incl.astype(F32), gates, precision=HIGHEST, preferred_element_type=F32)
            tot_all = gc_all[0:1, :] if reverse else gc_all[c - 1:c, :]
            for h in range(GDN_HEADS):
                hs = slice(h * HEAD_DIM, (h + 1) * HEAD_DIM)
                chains.append(dict(
                    qkv=qkv_ref, h=h, hs=hs, g=g, o_ref=outs[reverse], incl=incl, strict=strict,
                    state=(2 * g + reverse) * GDN_HEADS + h,
                    gc=gc_all[:, lane0 + h:lane0 + h + 1],
                    beta=gates[:, lane0 + GDN_HEADS + h:lane0 + GDN_HEADS + h + 1],
                    tot=tot_all[:, lane0 + h:lane0 + h + 1]))

    def head(ch, part):
        return ch["qkv"][:, part * GDN_WIDTH + ch["h"] * HEAD_DIM:part * GDN_WIDTH + (ch["h"] + 1) * HEAD_DIM]

    for ch in chains:
        gcb = jnp.broadcast_to(ch["gc"], (c, c))
        gc_row = jnp.sum(jnp.where(eye, gcb, 0.0), axis=0, keepdims=True)
        ch["gamma"] = jnp.where(ch["incl"], jnp.exp(jnp.where(ch["incl"], gcb - gc_row, 0.0)), 0.0)
        ch["egc"] = jnp.exp(ch["gc"])
        q, k = head(ch, 0), head(ch, 1)
        ch["kb"] = k * ch["beta"]
        ch["kq"] = _dot_bf16(jnp.concatenate([ch["kb"], q], axis=0), k, ((1,), (1,)))
    for ch in chains:
        ch["a"] = jnp.where(ch["strict"], ch["kq"][:c] * ch["gamma"], 0.0)
        ch["qk"] = jnp.where(ch["incl"], ch["kq"][c:] * ch["gamma"], 0.0).astype(BF16)
        x = jnp.where(diag_blk, -ch["a"], 0.0)
        ch["inv"] = eye_f + x
        ch["pw"] = _dot_3x(x, x)
    n_sq = INV_BASE.bit_length() - 1
    for step in range(1, n_sq):
        for ch in chains:
            if step < n_sq - 1:
                r = _dot_3x(jnp.concatenate([ch["inv"], ch["pw"]], axis=0), ch["pw"])
                ch["inv"], ch["pw"] = ch["inv"] + r[:c], r[c:]
            else:
                ch["inv"] = ch["inv"] + _dot_3x(ch["inv"], ch["pw"])
    for off in merge_masks:
        for ch in chains:
            ch["tmp"] = _dot_bf16(ch["inv"], jnp.where(off, ch["a"], 0.0))
        for ch in chains:
            ch["inv"] = ch["inv"] - _dot_bf16(ch["tmp"], ch["inv"])
    for ch in chains:
        rhs = jnp.concatenate([head(ch, 2) * ch["beta"], ch["kb"] * ch["egc"]], axis=1)
        ch["uw"] = _dot_bf16(ch["inv"], rhs)
    for ch in chains:
        ch["s"] = s_ref[ch["state"]]
        lhs = jnp.concatenate([ch["uw"][:, HEAD_DIM:], head(ch, 0) * ch["egc"]], axis=0)
        ch["wq"] = _dot_bf16(lhs, ch["s"])
    for ch in chains:
        vb = (ch["uw"][:, :HEAD_DIM] - ch["wq"][:c]).astype(BF16)
        ch["o_ref"][ch["g"], :, ch["hs"]] = ch["wq"][c:] + jnp.dot(ch["qk"], vb, preferred_element_type=F32)
        k_dec = head(ch, 1) * jnp.exp(ch["tot"] - ch["gc"])
        s_ref[ch["state"]] = ch["s"] * jnp.exp(ch["tot"]) + _dot_bf16(k_dec, vb, ((0,), (0,)))


def _gdn_scan(qkv, gates, bsz, seq, ctx_len):
    t_all = qkv.shape[0]
    c = GDN_CHUNK
    n_ctx, n_x = ctx_len // c, seq // c
    n_steps = n_ctx + n_x
    ctx_base = bsz * n_x
    n_b = GDN_BATCHES if bsz % GDN_BATCHES == 0 else 1

    def fwd_idx(g):
        def idx(bg, s):
            b = bg * n_b + g
            return (jnp.where(s < n_ctx, ctx_base + b * n_ctx + s, b * n_x + s - n_ctx), 0)
        return idx

    def bwd_idx(g):
        def idx(bg, s):
            b = bg * n_b + g
            return (jnp.where(s < n_ctx, ctx_base + b * n_ctx + n_ctx - 1 - s, b * n_x + n_steps - 1 - s), 0)
        return idx

    qw = 3 * GDN_WIDTH
    in_specs = []
    for g in range(n_b):
        in_specs += [pl.BlockSpec((c, qw), fwd_idx(g)), pl.BlockSpec((c, GATE_PAD), fwd_idx(g)),
                     pl.BlockSpec((c, qw), bwd_idx(g)), pl.BlockSpec((c, GATE_PAD), bwd_idx(g))]
    plane_ctx_base = (bsz // n_b) * n_x
    out_f = lambda bg, s: (0, jnp.where(s < n_ctx, plane_ctx_base + bg * n_ctx + s, bg * n_x + s - n_ctx), 0)
    out_b = lambda bg, s: (0, jnp.where(s < n_ctx, plane_ctx_base + bg * n_ctx + n_ctx - 1 - s,
                                        bg * n_x + n_steps - 1 - s), 0)
    return pl.pallas_call(
        functools.partial(_gdn_kernel, n_b),
        grid=(bsz // n_b, n_steps),
        in_specs=in_specs,
        out_specs=[pl.BlockSpec((n_b, c, GDN_WIDTH), out_f), pl.BlockSpec((n_b, c, GDN_WIDTH), out_b)],
        out_shape=[jax.ShapeDtypeStruct((n_b, t_all // n_b, GDN_WIDTH), F32)] * 2,
        scratch_shapes=[pltpu.VMEM((2 * n_b * GDN_HEADS, HEAD_DIM, HEAD_DIM), F32)],
        compiler_params=_params(2),
        name="gdn_scan",
    )(*([qkv, gates, qkv, gates] * n_b))


def _out_stage_kernel(add_pos, *refs):
    if add_pos:
        x_ref, pos_ref, of_ref, ob_ref, z_ref, mx_ref, og_ref, g1_ref, w_ref, o_ref = refs
    else:
        x_ref, of_ref, ob_ref, z_ref, mx_ref, og_ref, g1_ref, w_ref, o_ref = refs
    o = of_ref[...] + ob_ref[...]
    zs = z_ref[...]
    parts = []
    for h in range(GDN_HEADS):
        hs = slice(h * HEAD_DIM, (h + 1) * HEAD_DIM)
        oh = o[:, hs]
        oh = oh * lax.rsqrt(jnp.mean(oh * oh, axis=-1, keepdims=True) + EPS) * og_ref[...]
        parts.append((oh * zs[:, hs]).astype(BF16))
    parts.append(mx_ref[...].astype(BF16))
    y = jnp.dot(jnp.concatenate(parts, axis=-1), w_ref[...], preferred_element_type=F32)
    x = x_ref[...]
    if add_pos:
        x = x + pos_ref[...]
    o_ref[...] = x + g1_ref[0] * y


def _out_stage(xa, pos_ext, o_f, o_b, zs, mx, o_norm_g, g1, w_out, n_rows, bsz, seq, ctx_len):
    d = xa.shape[1]
    nx_tiles, tps, ctps = bsz * seq // TM, seq // TM, ctx_len // TM
    n_b = o_f.shape[0]
    add_pos = pos_ext is not None
    row = lambda w: pl.BlockSpec((TM, w), lambda j: (j, 0))

    def scan_idx(j):
        jc = j - nx_tiles
        b = jnp.where(j < nx_tiles, j // tps, jc // ctps)
        tile = jnp.where(j < nx_tiles, (b // n_b) * tps + j % tps,
                         (bsz // n_b) * tps + (b // n_b) * ctps + jc % ctps)
        return (b % n_b, tile, 0)

    scan = pl.BlockSpec((None, TM, GDN_WIDTH), scan_idx)
    in_specs, args = [row(d)], [xa]
    if add_pos:
        in_specs.append(pl.BlockSpec((TM, d), lambda j: (jnp.where(j < nx_tiles, j % tps, tps), 0)))
        args.append(pos_ext)
    in_specs += [scan, scan, row(GDN_WIDTH), row(GMLP_WIDTH),
                 pl.BlockSpec((1, HEAD_DIM), lambda j: (0, 0)),
                 pl.BlockSpec((1, 1, d), lambda j: (jnp.where(j < nx_tiles, j // tps, bsz), 0, 0)),
                 pl.BlockSpec((GDN_WIDTH + GMLP_WIDTH, d), lambda j: (0, 0))]
    args += [o_f, o_b, zs, mx, o_norm_g, g1, w_out]
    return pl.pallas_call(
        functools.partial(_out_stage_kernel, add_pos),
        grid=(n_rows // TM,),
        in_specs=in_specs,
        out_specs=row(d),
        out_shape=jax.ShapeDtypeStruct((n_rows, d), F32),
        compiler_params=_params(1),
        name="out_stage",
    )(*args)


def _swiglu_body(hb, w1, w3, w2):
    d_ff = w1.shape[1]
    acc = None
    for f0 in range(0, d_ff, FC):
        a = jnp.dot(hb, w1[:, f0:f0 + FC], preferred_element_type=F32)
        b = jnp.dot(hb, w3[:, f0:f0 + FC], preferred_element_type=F32)
        part = jnp.dot((_silu(a) * b).astype(BF16), w2[f0:f0 + FC, :], preferred_element_type=F32)
        acc = part if acc is None else acc + part
    return acc


def _ffn_kernel(x_ref, g_ref, sh_ref, sc_ref, g2_ref, w1_ref, w3_ref, w2_ref, o_ref):
    x = x_ref[...]
    h = _rms_mod(x, g_ref[...], sh_ref[0], sc_ref[0])
    o_ref[...] = x + g2_ref[0] * _swiglu_body(h.astype(BF16), w1_ref, w3_ref, w2_ref)


def _dense_ffn(xa, norm_g, sh, sc, g2, w1, w3, w2, bsz, seq):
    t_all, d = xa.shape
    d_ff = w1.shape[1]
    nx_tiles, tps = bsz * seq // TF, seq // TF
    mod_idx = lambda j: (jnp.where(j < nx_tiles, j // tps, bsz), 0, 0)
    resident = lambda shape: pl.BlockSpec(shape, lambda j: (0, 0), pipeline_mode=pl.Buffered(1))
    return pl.pallas_call(
        _ffn_kernel,
        grid=(t_all // TF,),
        in_specs=[pl.BlockSpec((TF, d), lambda j: (j, 0)),
                  pl.BlockSpec((1, d), lambda j: (0, 0)),
                  pl.BlockSpec((1, 1, d), mod_idx), pl.BlockSpec((1, 1, d), mod_idx),
                  pl.BlockSpec((1, 1, d), mod_idx),
                  resident((d, d_ff)), resident((d, d_ff)), resident((d_ff, d))],
        out_specs=pl.BlockSpec((TF, d), lambda j: (j, 0)),
        out_shape=jax.ShapeDtypeStruct((t_all, d), F32),
        compiler_params=_params(1),
        name="dense_ffn",
    )(xa, norm_g, sh, sc, g2, w1, w3, w2)


def _router_kernel(x_ref, g_ref, sh_ref, sc_ref, rw_ref, rb_ref, h_ref, route_ref, cnt_ref, run_scr):
    i = pl.program_id(0)

    @pl.when(i == 0)
    def _():
        run_scr[...] = jnp.zeros_like(run_scr)

    h = _rms_mod(x_ref[...], g_ref[...], sh_ref[0], sc_ref[0])
    h_ref[...] = h
    logits = jnp.dot(h, rw_ref[...], precision=HIGHEST, preferred_element_type=F32) + rb_ref[...]
    lane = lax.broadcasted_iota(jnp.int32, logits.shape, 1)
    m1 = jnp.max(logits, axis=-1, keepdims=True)
    i1 = jnp.min(jnp.where(logits == m1, lane, LANES), axis=-1, keepdims=True)
    rest = jnp.where(lane == i1, -jnp.inf, logits)
    m2 = jnp.max(rest, axis=-1, keepdims=True)
    i2 = jnp.min(jnp.where(rest == m2, lane, LANES), axis=-1, keepdims=True)
    e21 = jnp.exp(m2 - m1)
    p1 = 1.0 / (1.0 + e21)
    p2 = e21 * p1
    oh1 = (lane == i1).astype(F32)
    oh2 = (lane == i2).astype(F32)
    oh = oh1 + oh2
    rows = h.shape[0]
    ri = lax.broadcasted_iota(jnp.int32, (rows, rows), 0)
    ci = lax.broadcasted_iota(jnp.int32, (rows, rows), 1)
    earlier = (ci < ri).astype(BF16)
    before = jnp.dot(earlier, oh.astype(BF16), preferred_element_type=F32) + run_scr[...]
    r1 = jnp.sum(oh1 * before, axis=-1, keepdims=True)
    r2 = jnp.sum(oh2 * before, axis=-1, keepdims=True)
    run = run_scr[...] + jnp.sum(oh, axis=0, keepdims=True)
    run_scr[...] = run
    cnt_ref[...] = run
    vals = (p1, p2, i1.astype(F32), i2.astype(F32), r1, r2)
    out = jnp.zeros_like(logits)
    for n, val in enumerate(vals):
        out = jnp.where(lane == n, val, out)
    route_ref[...] = out


def _router(x1, norm_g, sh, sc, rw_pad, rb_pad, seq):
    n, d = x1.shape
    tps = seq // TF
    mod_idx = lambda j: (j // tps, 0, 0)
    return pl.pallas_call(
        _router_kernel,
        grid=(n // TF,),
        in_specs=[pl.BlockSpec((TF, d), lambda j: (j, 0)),
                  pl.BlockSpec((1, d), lambda j: (0, 0)),
                  pl.BlockSpec((1, 1, d), mod_idx), pl.BlockSpec((1, 1, d), mod_idx),
                  pl.BlockSpec((d, LANES), lambda j: (0, 0)),
                  pl.BlockSpec((1, LANES), lambda j: (0, 0))],
        out_specs=[pl.BlockSpec((TF, d), lambda j: (j, 0)),
                   pl.BlockSpec((TF, LANES), lambda j: (j, 0)),
                   pl.BlockSpec((1, LANES), lambda j: (0, 0))],
        out_shape=[jax.ShapeDtypeStruct((n, d), F32),
                   jax.ShapeDtypeStruct((n, LANES), F32),
                   jax.ShapeDtypeStruct((1, LANES), F32)],
        scratch_shapes=[pltpu.VMEM((1, LANES), F32)],
        compiler_params=_params(1),
        name="router",
    )(x1, norm_g, sh, sc, rw_pad, rb_pad)


def _dispatch_kernel(dest_ref, h_ref, xb_in_ref, xb_ref, sem):
    del xb_in_ref
    rows = h_ref.shape[0]

    def issue(t, carry):
        for kk in range(2):
            pltpu.make_async_copy(h_ref.at[pl.ds(t, 1)], xb_ref.at[pl.ds(dest_ref[0, 0, kk * rows + t], 1)],
                                  sem).start()
        return carry

    lax.fori_loop(0, rows, issue, 0)
    for kk in range(2):
        pltpu.make_async_copy(h_ref, xb_ref.at[pl.ds(0, rows)], sem).wait()


def _dispatch(hx, dest_tiles, xb_init):
    n, d = hx.shape
    return pl.pallas_call(
        _dispatch_kernel,
        grid=(n // TF,),
        in_specs=[pl.BlockSpec((1, 1, 2 * TF), lambda j: (j, 0, 0), memory_space=pltpu.SMEM),
                  pl.BlockSpec((TF, d), lambda j: (j, 0)),
                  pl.BlockSpec(memory_space=pl.ANY)],
        out_specs=pl.BlockSpec(memory_space=pl.ANY),
        out_shape=jax.ShapeDtypeStruct(xb_init.shape, F32),
        scratch_shapes=[pltpu.SemaphoreType.DMA(())],
        input_output_aliases={2: 0},
        compiler_params=_params(1),
        name="dispatch",
    )(dest_tiles, hx, xb_init)


def _grouped_kernel(be_ref, nb_ref, x_ref, w1_ref, w3_ref, w2_ref, o_ref):
    del be_ref

    @pl.when(pl.program_id(0) < nb_ref[0])
    def _():
        o_ref[...] = _swiglu_body(x_ref[...].astype(BF16), w1_ref.at[0], w3_ref.at[0], w2_ref.at[0])


def _grouped_swiglu(xb, block_e, nb_used, w1, w3, w2):
    r, d = xb.shape
    d_ff = w1.shape[2]
    row_idx = lambda i, be, nb: (jnp.minimum(i, nb[0] - 1), 0)
    w_idx = lambda i, be, nb: (be[i], 0, 0)
    expert = lambda shape: pl.BlockSpec(shape, w_idx, pipeline_mode=pl.Buffered(1))
    return pl.pallas_call(
        _grouped_kernel,
        grid_spec=pltpu.PrefetchScalarGridSpec(
            num_scalar_prefetch=2,
            grid=(r // MOE_ROWS,),
            in_specs=[pl.BlockSpec((MOE_ROWS, d), row_idx),
                      expert((1, d, d_ff)), expert((1, d, d_ff)), expert((1, d_ff, d))],
            out_specs=pl.BlockSpec((MOE_ROWS, d), row_idx)),
        out_shape=jax.ShapeDtypeStruct((r, d), F32),
        input_output_aliases={2: 0},
        compiler_params=_params(1),
        name="grouped_swiglu",
    )(block_e, nb_used, xb, w1, w3, w2)


def _combine_kernel(dest_ref, x_ref, route_ref, g2_ref, fg_ref, yb_ref, o_ref, buf, sem):
    rows = x_ref.shape[0]

    def issue(t, carry):
        for kk in range(2):
            pltpu.make_async_copy(yb_ref.at[pl.ds(dest_ref[0, 0, kk * rows + t], 1)],
                                  buf.at[kk, pl.ds(t, 1)], sem).start()
        return carry

    lax.fori_loop(0, rows, issue, 0)
    for kk in range(2):
        pltpu.make_async_copy(yb_ref.at[pl.ds(0, rows)], buf.at[kk], sem).wait()
    route = route_ref[...]
    y = route[:, 0:1] * buf[0] + route[:, 1:2] * buf[1]
    xn = x_ref[...] + g2_ref[0] * y
    o_ref[...] = xn * lax.rsqrt(jnp.mean(xn * xn, axis=-1, keepdims=True) + EPS) * fg_ref[...]


def _combine(x1, route, dest_tiles, g2, final_g, yb, seq):
    n, d = x1.shape
    tps = seq // TM
    return pl.pallas_call(
        _combine_kernel,
        grid=(n // TM,),
        in_specs=[pl.BlockSpec((1, 1, 2 * TM), lambda j: (j, 0, 0), memory_space=pltpu.SMEM),
                  pl.BlockSpec((TM, d), lambda j: (j, 0)),
                  pl.BlockSpec((TM, LANES), lambda j: (j, 0)),
                  pl.BlockSpec((1, 1, d), lambda j: (j // tps, 0, 0)),
                  pl.BlockSpec((1, d), lambda j: (0, 0)),
                  pl.BlockSpec(memory_space=pl.ANY)],
        out_specs=pl.BlockSpec((TM, d), lambda j: (j, 0)),
        out_shape=jax.ShapeDtypeStruct((n, d), F32),
        scratch_shapes=[pltpu.VMEM((2, TM, d), F32), pltpu.SemaphoreType.DMA(())],
        compiler_params=_params(1),
        name="combine",
    )(dest_tiles, x1, route, g2, final_g, yb)


def _moe_layer(x1, norm_g, sh, sc, g2, router_w, router_b, w1, w3, w2, final_g, seq):
    n, d = x1.shape
    rw_pad = jnp.zeros((d, LANES), F32).at[:, :N_EXPERTS].set(router_w)
    rb_pad = jnp.full((1, LANES), -1e30, F32).at[0, :N_EXPERTS].set(router_b)
    hx, route, counts = _router(x1, norm_g, sh, sc, rw_pad, rb_pad, seq)

    cnt = counts[0, :N_EXPERTS].astype(jnp.int32)
    padded = (cnt + MOE_ROWS - 1) // MOE_ROWS * MOE_ROWS
    seg_end = jnp.cumsum(padded)
    pad_start = seg_end - padded
    n_blocks = (2 * n) // MOE_ROWS + N_EXPERTS
    e12 = route[:, 2:4].astype(jnp.int32)
    r12 = route[:, 4:6].astype(jnp.int32)
    dest = pad_start[e12] + r12
    nb_used = (seg_end[-1] // MOE_ROWS).astype(jnp.int32).reshape(1)
    blk = jnp.minimum(jnp.arange(n_blocks, dtype=jnp.int32), nb_used[0] - 1) * MOE_ROWS
    block_e = jnp.minimum(jnp.searchsorted(seg_end, blk, side="right"), N_EXPERTS - 1).astype(jnp.int32)

    def tiles(rows):
        return dest.reshape(n // rows, rows, 2).transpose(0, 2, 1).reshape(n // rows, 1, 2 * rows)

    xb = _dispatch(hx, tiles(TF), jnp.zeros((n_blocks * MOE_ROWS, d), F32))
    yb = _grouped_swiglu(xb, block_e, nb_used, w1, w3, w2)
    return _combine(x1, route, tiles(TM), g2, final_g, yb, seq)


def _grid_pos_embed(rows, dim):
    row = jnp.broadcast_to(jnp.arange(rows, dtype=F32)[:, None], (rows, GRID_W)).reshape(-1)
    col = jnp.broadcast_to(jnp.arange(GRID_W, dtype=F32)[None, :], (rows, GRID_W)).reshape(-1)
    quarter = dim // 4
    omega = 1.0 / (10000.0 ** (jnp.arange(quarter, dtype=F32) / quarter))

    def enc(p):
        ang = p[:, None] * omega[None, :]
        return jnp.concatenate([jnp.sin(ang), jnp.cos(ang)], axis=-1)

    return jnp.concatenate([enc(row), enc(col)], axis=-1)


def _gate_lane_vec(t):
    v = jnp.zeros((2, 2 * GDN_HEADS), F32).at[:, :GDN_HEADS].set(t.astype(F32)).reshape(1, -1)
    return jnp.pad(v, ((0, 0), (0, GATE_PAD - v.shape[1])))


def kernel(x, c, ctx, c_ctx, w_mod, b_mod, norm1_g, norm2_g, w_in, conv_w, a_log, dt_bias, o_norm_g, sgu_norm_g, w_s, b_s, w_out, ffn_w1, ffn_w3, ffn_w2, router_w, router_b, exp_w1, exp_w3, exp_w2, final_g):
    bsz, seq, d = x.shape
    ctx_len = ctx.shape[1]
    depth = w_mod.shape[0]
    assert seq % TF == 0 and ctx_len % TM == 0 and (bsz * ctx_len) % TF == 0 and depth == 2
    n_lat = bsz * seq

    cond_rows = -(-(bsz + 1) // SUBLANES) * SUBLANES
    cond = jnp.zeros((cond_rows, d), F32).at[:bsz].set(c).at[bsz].set(c_ctx)
    mod = _modulation(cond, w_mod, b_mod)[:, :bsz + 1].reshape(depth, bsz + 1, 6, 1, d)

    pos_ext = jnp.concatenate([_grid_pos_embed(seq // GRID_W, d), jnp.zeros((TM, d), F32)], axis=0)
    xa = jnp.concatenate([x.reshape(n_lat, d), ctx.reshape(bsz * ctx_len, d)], axis=0)

    qkvz_w = 4 * GDN_WIDTH
    n_gate = 4 * GDN_HEADS
    out = None
    for layer in range(depth):
        last = layer == depth - 1
        sh1, sc1, g1, sh2, sc2, g2 = (mod[layer, :, i] for i in range(6))
        wl = w_in[layer]
        w_in_r = jnp.concatenate([wl[:, :qkvz_w], wl[:, qkvz_w + n_gate:], wl[:, qkvz_w:qkvz_w + n_gate],
                                  jnp.zeros((d, GATE_PAD - n_gate), F32)], axis=1).astype(BF16)
        b_s_b = jnp.broadcast_to(b_s[layer][:, :, None], (GMLP_GROUPS, GMLP_CHUNK, GMLP_GROUP_DIM))
        pos = pos_ext if layer == 0 else None
        qkv, zs, gates, mx = _in_stage(
            xa, pos, norm1_g[layer][None], sh1, sc1, w_in_r, conv_w[layer],
            _gate_lane_vec(a_log[layer]), _gate_lane_vec(dt_bias[layer]), sgu_norm_g[layer][None],
            w_s[layer].astype(BF16), b_s_b, bsz, seq, ctx_len)
        o_f, o_b = _gdn_scan(qkv, gates, bsz, seq, ctx_len)
        n_rows = n_lat if last else xa.shape[0]
        xa = _out_stage(xa, pos, o_f, o_b, zs, mx, o_norm_g[layer][None], g1, w_out[layer].astype(BF16),
                        n_rows, bsz, seq, ctx_len)
        i = layer // 2
        if layer % 2 == 0:
            xa = _dense_ffn(xa, norm2_g[layer][None], sh2, sc2, g2, ffn_w1[i].astype(BF16),
                            ffn_w3[i].astype(BF16), ffn_w2[i].astype(BF16), bsz, seq)
        else:
            out = _moe_layer(xa, norm2_g[layer][None], sh2, sc2, g2, router_w[i], router_b[i],
                             exp_w1[i].astype(BF16), exp_w3[i].astype(BF16), exp_w2[i].astype(BF16),
                             final_g[None], seq)
    return out.reshape(bsz, seq, d)
```

```python
import functools

import jax
import jax.numpy as jnp
from jax import lax
from jax.experimental import pallas as pl
from jax.experimental.pallas import tpu as pltpu

F32 = jnp.float32
BF16 = jnp.bfloat16
EPS = 1e-6
HIGHEST = lax.Precision.HIGHEST

HEAD_DIM = 128
GDN_HEADS = 4
GDN_WIDTH = GDN_HEADS * HEAD_DIM
GDN_CHUNK = 64
CONV_K = 5
GMLP_GROUPS = 4
GMLP_GROUP_DIM = 128
GMLP_WIDTH = GMLP_GROUPS * GMLP_GROUP_DIM
GMLP_CHUNK = 128
GRID_W = 64
N_EXPERTS = 8

LANES = 128
SUBLANES = 8
VMEM_LIMIT = 56 * 1024 * 1024

TM = 256
HALO = SUBLANES
TF = 512
MOE_ROWS = 512
FC = 512
GATE_PAD = LANES


def _sigmoid(x):
    return 1.0 / (1.0 + jnp.exp(-x))


def _silu(x):
    return x * _sigmoid(x)


def _gelu_tanh(x):
    c = 0.7978845608028654
    return 0.5 * x * (1.0 + jnp.tanh(c * (x + 0.044715 * (x * x * x))))


def _softplus(x):
    return jnp.maximum(x, 0.0) + jnp.log1p(jnp.exp(-jnp.abs(x)))


def _rms_mod(x, g, shift, scale):
    ms = jnp.mean(x * x, axis=-1, keepdims=True)
    return (x * lax.rsqrt(ms + EPS) * g) * (1.0 + scale) + shift


def _params(n_axes=1):
    return pltpu.CompilerParams(dimension_semantics=("arbitrary",) * n_axes,
                                vmem_limit_bytes=VMEM_LIMIT)


def _mod_kernel(c_ref, w_ref, b_ref, o_ref):
    s = _silu(c_ref[...])
    o_ref[0] = jnp.dot(s, w_ref[0], precision=HIGHEST, preferred_element_type=F32) + b_ref[0]


def _modulation(cond, w_mod, b_mod):
    depth, d, six_d = w_mod.shape
    rows = cond.shape[0]
    tn = 1536
    return pl.pallas_call(
        _mod_kernel,
        grid=(depth, six_d // tn),
        in_specs=[pl.BlockSpec((rows, d), lambda l, n: (0, 0)),
                  pl.BlockSpec((1, d, tn), lambda l, n: (l, 0, n)),
                  pl.BlockSpec((1, 1, tn), lambda l, n: (l, 0, n))],
        out_specs=pl.BlockSpec((1, rows, tn), lambda l, n: (l, 0, n)),
        out_shape=jax.ShapeDtypeStruct((depth, rows, six_d), F32),
        compiler_params=_params(2),
        name="modulation",
    )(cond, w_mod, b_mod.reshape(depth, 1, six_d))


def _in_stage_kernel(add_pos, nx_tiles, tps, ctps, *refs):
    if add_pos:
        (x_ref, xp_ref, xn_ref, pos_ref, posp_ref, posn_ref, g_ref, sh_ref, sc_ref, w_ref, cw_ref,
         alog_ref, dt_ref, sg_ref, ws_ref, bs_ref, qkv_ref, z_ref, gate_ref, mx_ref, pe_scr) = refs
    else:
        (x_ref, xp_ref, xn_ref, g_ref, sh_ref, sc_ref, w_ref, cw_ref,
         alog_ref, dt_ref, sg_ref, ws_ref, bs_ref, qkv_ref, z_ref, gate_ref, mx_ref, pe_scr) = refs
    j = pl.program_id(0)
    is_x = j < nx_tiles
    tis = jnp.where(is_x, j % tps, (j - nx_tiles) % ctps)
    n_in_seq = jnp.where(is_x, tps, ctps)
    first = tis == 0
    last = tis == n_in_seq - 1

    xm, xp, xn = x_ref[...], xp_ref[...], xn_ref[...]
    if add_pos:
        xm, xp, xn = xm + pos_ref[...], xp + posp_ref[...], xn + posn_ref[...]
    xe = jnp.concatenate([xp, xm, xn], axis=0)
    h = _rms_mod(xe, g_ref[...], sh_ref[0], sc_ref[0])
    p = jnp.dot(h.astype(BF16), w_ref[...], preferred_element_type=F32)

    qkv_w = 3 * GDN_WIDTH
    rows = lax.broadcasted_iota(jnp.int32, (TM + 2 * HALO, 1), 0)
    valid = jnp.logical_and(jnp.logical_or(rows >= HALO, jnp.logical_not(first)),
                            jnp.logical_or(rows < TM + HALO, jnp.logical_not(last)))
    pe_scr[...] = jnp.where(valid, p[:, :qkv_w], 0.0)
    acc = cw_ref[0:1, :] * pe_scr[pl.ds(HALO - CONV_K // 2, TM), :]
    for t in range(1, CONV_K):
        acc = acc + cw_ref[t:t + 1, :] * pe_scr[pl.ds(HALO - CONV_K // 2 + t, TM), :]
    qkv = _silu(acc)
    for i in range(3 * GDN_HEADS):
        blk = qkv[:, i * HEAD_DIM:(i + 1) * HEAD_DIM]
        if i < 2 * GDN_HEADS:
            blk = blk * lax.rsqrt(jnp.sum(blk * blk, axis=-1, keepdims=True) + EPS)
        if i < GDN_HEADS:
            blk = blk * (HEAD_DIM ** -0.5)
        qkv_ref[:, i * HEAD_DIM:(i + 1) * HEAD_DIM] = blk

    pm = p[HALO:HALO + TM, :]
    z_ref[...] = _silu(pm[:, qkv_w:qkv_w + GDN_WIDTH])

    c0 = qkv_w + GDN_WIDTH
    gu = _gelu_tanh(pm[:, c0:c0 + GMLP_WIDTH])
    gvr = _gelu_tanh(pm[:, c0 + GMLP_WIDTH:c0 + 2 * GMLP_WIDTH])
    mu = jnp.mean(gvr, axis=-1, keepdims=True)
    xc = gvr - mu
    gv = xc * lax.rsqrt(jnp.mean(xc * xc, axis=-1, keepdims=True) + EPS) * sg_ref[...]
    for r in range(TM // GMLP_CHUNK):
        rs = slice(r * GMLP_CHUNK, (r + 1) * GMLP_CHUNK)
        for g in range(GMLP_GROUPS):
            cs = slice(g * GMLP_GROUP_DIM, (g + 1) * GMLP_GROUP_DIM)
            mixed = jnp.dot(ws_ref[g], gv[rs, cs].astype(BF16), preferred_element_type=F32) + bs_ref[g]
            mx_ref[rs, cs] = gu[rs, cs] * mixed

    graw = pm[:, c0 + 2 * GMLP_WIDTH:]
    lane = lax.broadcasted_iota(jnp.int32, graw.shape, 1)
    is_decay = (lane % (2 * GDN_HEADS)) < GDN_HEADS
    decay = -jnp.exp(alog_ref[...]) * _softplus(graw + dt_ref[...])
    gate_ref[...] = jnp.where(is_decay, decay, _sigmoid(graw))


def _in_stage(xa, pos_ext, norm_g, sh, sc, w_in_r, conv_w, alog_vec, dt_vec, sgu_g, w_s, b_s_b,
              bsz, seq, ctx_len):
    t_all, d = xa.shape
    nx_tiles = bsz * seq // TM
    tps, ctps = seq // TM, ctx_len // TM
    n_tiles = t_all // TM
    hb = TM // HALO
    add_pos = pos_ext is not None
    pw = w_in_r.shape[1]

    def mod_idx(j):
        return (jnp.where(j < nx_tiles, j // tps, bsz), 0, 0)

    in_specs = [pl.BlockSpec((TM, d), lambda j: (j, 0)),
                pl.BlockSpec((HALO, d), lambda j: (jnp.maximum(j * hb - 1, 0), 0)),
                pl.BlockSpec((HALO, d), lambda j: (jnp.minimum((j + 1) * hb, t_all // HALO - 1), 0))]
    args = [xa, xa, xa]
    if add_pos:
        in_specs += [
            pl.BlockSpec((TM, d), lambda j: (jnp.where(j < nx_tiles, j % tps, tps), 0)),
            pl.BlockSpec((HALO, d), lambda j: (jnp.where(j < nx_tiles, jnp.maximum((j % tps) * hb - 1, 0),
                                                         seq // HALO), 0)),
            pl.BlockSpec((HALO, d), lambda j: (jnp.where(j < nx_tiles, (j % tps + 1) * hb, seq // HALO), 0))]
        args += [pos_ext, pos_ext, pos_ext]
    const2 = lambda j: (0, 0)
    const3 = lambda j: (0, 0, 0)
    in_specs += [pl.BlockSpec((1, d), const2),
                 pl.BlockSpec((1, 1, d), mod_idx), pl.BlockSpec((1, 1, d), mod_idx),
                 pl.BlockSpec((d, pw), const2),
                 pl.BlockSpec((CONV_K, 3 * GDN_WIDTH), const2),
                 pl.BlockSpec((1, GATE_PAD), const2), pl.BlockSpec((1, GATE_PAD), const2),
                 pl.BlockSpec((1, GMLP_WIDTH), const2),
                 pl.BlockSpec((GMLP_GROUPS, GMLP_CHUNK, GMLP_CHUNK), const3),
                 pl.BlockSpec((GMLP_GROUPS, GMLP_CHUNK, GMLP_GROUP_DIM), const3)]
    args += [norm_g, sh, sc, w_in_r, conv_w, alog_vec, dt_vec, sgu_g, w_s, b_s_b]
    widths = (3 * GDN_WIDTH, GDN_WIDTH, GATE_PAD, GMLP_WIDTH)
    return pl.pallas_call(
        functools.partial(_in_stage_kernel, add_pos, nx_tiles, tps, ctps),
        grid=(n_tiles,),
        in_specs=in_specs,
        out_specs=[pl.BlockSpec((TM, w), lambda j: (j, 0)) for w in widths],
        out_shape=[jax.ShapeDtypeStruct((t_all, w), F32) for w in widths],
        scratch_shapes=[pltpu.VMEM((TM + 2 * HALO, 3 * GDN_WIDTH), F32)],
        compiler_params=_params(1),
        name="in_stage",
    )(*args)


INV_BASE = 16


GDN_BATCHES = 4


def _dot_bf16(a, b, dims=None):
    a, b = a.astype(BF16), b.astype(BF16)
    if dims is None:
        return jnp.dot(a, b, preferred_element_type=F32)
    return lax.dot_general(a, b, (dims, ((), ())), preferred_element_type=F32)


def _gdn_kernel(n_b, *refs):
    ins, outs, s_ref = refs[:4 * n_b], refs[4 * n_b:4 * n_b + 2], refs[4 * n_b + 2]

    @pl.when(pl.program_id(1) == 0)
    def _():
        s_ref[...] = jnp.zeros_like(s_ref)

    c = GDN_CHUNK
    ii = lax.broadcasted_iota(jnp.int32, (c, c), 0)
    jj = lax.broadcasted_iota(jnp.int32, (c, c), 1)
    eye = ii == jj
    eye_f = eye.astype(F32)
    bi, bj = ii // INV_BASE, jj // INV_BASE
    diag_blk = bi == bj
    merge_masks = []
    size = INV_BASE
    while size < c:
        grp = 2 * size // INV_BASE
        merge_masks.append(jnp.logical_and(bi // grp == bj // grp, bi // (grp // 2) != bj // (grp // 2)))
        size *= 2

    chains = []
    for g in range(n_b):
        for reverse in (False, True):
            qkv_ref, gate_ref = ins[4 * g + 2 * reverse], ins[4 * g + 2 * reverse + 1]
            incl, strict = (jj >= ii, jj > ii) if reverse else (jj <= ii, jj < ii)
            lane0 = 2 * GDN_HEADS * reverse
            gates = gate_ref[...]
            gc_all = jnp.dot(incl.astype(F32), gates, precision=HIGHEST, preferred_element_type=F32)
            tot_all = gc_all[0:1, :] if reverse else gc_all[c - 1:c, :]
            for h in range(GDN_HEADS):
                hs = slice(h * HEAD_DIM, (h + 1) * HEAD_DIM)
                chains.append(dict(
                    qkv=qkv_ref, h=h, hs=hs, g=g, o_ref=outs[reverse], incl=incl, strict=strict,
                    state=(2 * g + reverse) * GDN_HEADS + h,
                    gc=gc_all[:, lane0 + h:lane0 + h + 1],
                    beta=gates[:, lane0 + GDN_HEADS + h:lane0 + GDN_HEADS + h + 1],
                    tot=tot_all[:, lane0 + h:lane0 + h + 1]))

    def head(ch, part):
        return ch["qkv"][:, part * GDN_WIDTH + ch["h"] * HEAD_DIM:part * GDN_WIDTH + (ch["h"] + 1) * HEAD_DIM]

    for ch in chains:
        gcb = jnp.broadcast_to(ch["gc"], (c, c))
        gc_row = jnp.sum(jnp.where(eye, gcb, 0.0), axis=0, keepdims=True)
        ch["gamma"] = jnp.where(ch["incl"], jnp.exp(jnp.where(ch["incl"], gcb - gc_row, 0.0)), 0.0)
        ch["egc"] = jnp.exp(ch["gc"])
        q, k = head(ch, 0), head(ch, 1)
        ch["kb"] = k * ch["beta"]
        ch["kq"] = _dot_bf16(jnp.concatenate([ch["kb"], q], axis=0), k, ((1,), (1,)))
    for ch in chains:
        ch["a"] = jnp.where(ch["strict"], ch["kq"][:c] * ch["gamma"], 0.0)
        ch["qk"] = jnp.where(ch["incl"], ch["kq"][c:] * ch["gamma"], 0.0).astype(BF16)
        x = jnp.where(diag_blk, -ch["a"], 0.0)
        ch["inv"] = eye_f + x
        ch["pw"] = _dot_bf16(x, x)
    n_sq = INV_BASE.bit_length() - 1
    for step in range(1, n_sq):
        for ch in chains:
            if step < n_sq - 1:
                r = _dot_bf16(jnp.concatenate([ch["inv"], ch["pw"]], axis=0), ch["pw"])
                ch["inv"], ch["pw"] = ch["inv"] + r[:c], r[c:]
            else:
                ch["inv"] = ch["inv"] + _dot_bf16(ch["inv"], ch["pw"])
    for off in merge_masks:
        for ch in chains:
            ch["tmp"] = _dot_bf16(ch["inv"], jnp.where(off, ch["a"], 0.0))
        for ch in chains:
            ch["inv"] = ch["inv"] - _dot_bf16(ch["tmp"], ch["inv"])
    for ch in chains:
        rhs = jnp.concatenate([head(ch, 2) * ch["beta"], ch["kb"] * ch["egc"]], axis=1)
        ch["uw"] = _dot_bf16(ch["inv"], rhs)
    for ch in chains:
        ch["s"] = s_ref[ch["state"]]
        lhs = jnp.concatenate([ch["uw"][:, HEAD_DIM:], head(ch, 0) * ch["egc"]], axis=0)
        ch["wq"] = _dot_bf16(lhs, ch["s"])
    for ch in chains:
        vb = (ch["uw"][:, :HEAD_DIM] - ch["wq"][:c]).astype(BF16)
        ch["o_ref"][ch["g"], :, ch["hs"]] = ch["wq"][c:] + jnp.dot(ch["qk"], vb, preferred_element_type=F32)
        k_dec = head(ch, 1) * jnp.exp(ch["tot"] - ch["gc"])
        s_ref[ch["state"]] = ch["s"] * jnp.exp(ch["tot"]) + _dot_bf16(k_dec, vb, ((0,), (0,)))


def _gdn_scan(qkv, gates, bsz, seq, ctx_len):
    t_all = qkv.shape[0]
    c = GDN_CHUNK
    n_ctx, n_x = ctx_len // c, seq // c
    n_steps = n_ctx + n_x
    ctx_base = bsz * n_x
    n_b = GDN_BATCHES if bsz % GDN_BATCHES == 0 else 1

    def fwd_idx(g):
        def idx(bg, s):
            b = bg * n_b + g
            return (jnp.where(s < n_ctx, ctx_base + b * n_ctx + s, b * n_x + s - n_ctx), 0)
        return idx

    def bwd_idx(g):
        def idx(bg, s):
            b = bg * n_b + g
            return (jnp.where(s < n_ctx, ctx_base + b * n_ctx + n_ctx - 1 - s, b * n_x + n_steps - 1 - s), 0)
        return idx

    qw = 3 * GDN_WIDTH
    in_specs = []
    for g in range(n_b):
        in_specs += [pl.BlockSpec((c, qw), fwd_idx(g)), pl.BlockSpec((c, GATE_PAD), fwd_idx(g)),
                     pl.BlockSpec((c, qw), bwd_idx(g)), pl.BlockSpec((c, GATE_PAD), bwd_idx(g))]
    plane_ctx_base = (bsz // n_b) * n_x
    out_f = lambda bg, s: (0, jnp.where(s < n_ctx, plane_ctx_base + bg * n_ctx + s, bg * n_x + s - n_ctx), 0)
    out_b = lambda bg, s: (0, jnp.where(s < n_ctx, plane_ctx_base + bg * n_ctx + n_ctx - 1 - s,
                                        bg * n_x + n_steps - 1 - s), 0)
    return pl.pallas_call(
        functools.partial(_gdn_kernel, n_b),
        grid=(bsz // n_b, n_steps),
        in_specs=in_specs,
        out_specs=[pl.BlockSpec((n_b, c, GDN_WIDTH), out_f), pl.BlockSpec((n_b, c, GDN_WIDTH), out_b)],
        out_shape=[jax.ShapeDtypeStruct((n_b, t_all // n_b, GDN_WIDTH), F32)] * 2,
        scratch_shapes=[pltpu.VMEM((2 * n_b * GDN_HEADS, HEAD_DIM, HEAD_DIM), F32)],
        compiler_params=_params(2),
        name="gdn_scan",
    )(*([qkv, gates, qkv, gates] * n_b))


def _out_stage_kernel(add_pos, *refs):
    if add_pos:
        x_ref, pos_ref, of_ref, ob_ref, z_ref, mx_ref, og_ref, g1_ref, w_ref, o_ref = refs
    else:
        x_ref, of_ref, ob_ref, z_ref, mx_ref, og_ref, g1_ref, w_ref, o_ref = refs
    o = of_ref[...] + ob_ref[...]
    zs = z_ref[...]
    parts = []
    for h in range(GDN_HEADS):
        hs = slice(h * HEAD_DIM, (h + 1) * HEAD_DIM)
        oh = o[:, hs]
        oh = oh * lax.rsqrt(jnp.mean(oh * oh, axis=-1, keepdims=True) + EPS) * og_ref[...]
        parts.append((oh * zs[:, hs]).astype(BF16))
    parts.append(mx_ref[...].astype(BF16))
    y = jnp.dot(jnp.concatenate(parts, axis=-1), w_ref[...], preferred_element_type=F32)
    x = x_ref[...]
    if add_pos:
        x = x + pos_ref[...]
    o_ref[...] = x + g1_ref[0] * y


def _out_stage(xa, pos_ext, o_f, o_b, zs, mx, o_norm_g, g1, w_out, n_rows, bsz, seq, ctx_len):
    d = xa.shape[1]
    nx_tiles, tps, ctps = bsz * seq // TM, seq // TM, ctx_len // TM
    n_b = o_f.shape[0]
    add_pos = pos_ext is not None
    row = lambda w: pl.BlockSpec((TM, w), lambda j: (j, 0))

    def scan_idx(j):
        jc = j - nx_tiles
        b = jnp.where(j < nx_tiles, j // tps, jc // ctps)
        tile = jnp.where(j < nx_tiles, (b // n_b) * tps + j % tps,
                         (bsz // n_b) * tps + (b // n_b) * ctps + jc % ctps)
        return (b % n_b, tile, 0)

    scan = pl.BlockSpec((None, TM, GDN_WIDTH), scan_idx)
    in_specs, args = [row(d)], [xa]
    if add_pos:
        in_specs.append(pl.BlockSpec((TM, d), lambda j: (jnp.where(j < nx_tiles, j % tps, tps), 0)))
        args.append(pos_ext)
    in_specs += [scan, scan, row(GDN_WIDTH), row(GMLP_WIDTH),
                 pl.BlockSpec((1, HEAD_DIM), lambda j: (0, 0)),
                 pl.BlockSpec((1, 1, d), lambda j: (jnp.where(j < nx_tiles, j // tps, bsz), 0, 0)),
                 pl.BlockSpec((GDN_WIDTH + GMLP_WIDTH, d), lambda j: (0, 0))]
    args += [o_f, o_b, zs, mx, o_norm_g, g1, w_out]
    return pl.pallas_call(
        functools.partial(_out_stage_kernel, add_pos),
        grid=(n_rows // TM,),
        in_specs=in_specs,
        out_specs=row(d),
        out_shape=jax.ShapeDtypeStruct((n_rows, d), F32),
        compiler_params=_params(1),
        name="out_stage",
    )(*args)


def _swiglu_body(hb, w1, w3, w2):
    d_ff = w1.shape[1]
    acc = None
    for f0 in range(0, d_ff, FC):
        a = jnp.dot(hb, w1[:, f0:f0 + FC], preferred_element_type=F32)
        b = jnp.dot(hb, w3[:, f0:f0 + FC], preferred_element_type=F32)
        part = jnp.dot((_silu(a) * b).astype(BF16), w2[f0:f0 + FC, :], preferred_element_type=F32)
        acc = part if acc is None else acc + part
    return acc


def _ffn_kernel(x_ref, g_ref, sh_ref, sc_ref, g2_ref, w1_ref, w3_ref, w2_ref, o_ref):
    x = x_ref[...]
    h = _rms_mod(x, g_ref[...], sh_ref[0], sc_ref[0])
    o_ref[...] = x + g2_ref[0] * _swiglu_body(h.astype(BF16), w1_ref, w3_ref, w2_ref)


def _dense_ffn(xa, norm_g, sh, sc, g2, w1, w3, w2, bsz, seq):
    t_all, d = xa.shape
    d_ff = w1.shape[1]
    nx_tiles, tps = bsz * seq // TF, seq // TF
    mod_idx = lambda j: (jnp.where(j < nx_tiles, j // tps, bsz), 0, 0)
    resident = lambda shape: pl.BlockSpec(shape, lambda j: (0, 0), pipeline_mode=pl.Buffered(1))
    return pl.pallas_call(
        _ffn_kernel,
        grid=(t_all // TF,),
        in_specs=[pl.BlockSpec((TF, d), lambda j: (j, 0)),
                  pl.BlockSpec((1, d), lambda j: (0, 0)),
                  pl.BlockSpec((1, 1, d), mod_idx), pl.BlockSpec((1, 1, d), mod_idx),
                  pl.BlockSpec((1, 1, d), mod_idx),
                  resident((d, d_ff)), resident((d, d_ff)), resident((d_ff, d))],
        out_specs=pl.BlockSpec((TF, d), lambda j: (j, 0)),
        out_shape=jax.ShapeDtypeStruct((t_all, d), F32),
        compiler_params=_params(1),
        name="dense_ffn",
    )(xa, norm_g, sh, sc, g2, w1, w3, w2)


def _router_kernel(x_ref, g_ref, sh_ref, sc_ref, rw_ref, rb_ref, h_ref, route_ref, cnt_ref, run_scr):
    i = pl.program_id(0)

    @pl.when(i == 0)
    def _():
        run_scr[...] = jnp.zeros_like(run_scr)

    h = _rms_mod(x_ref[...], g_ref[...], sh_ref[0], sc_ref[0])
    h_ref[...] = h
    logits = jnp.dot(h, rw_ref[...], precision=HIGHEST, preferred_element_type=F32) + rb_ref[...]
    lane = lax.broadcasted_iota(jnp.int32, logits.shape, 1)
    m1 = jnp.max(logits, axis=-1, keepdims=True)
    i1 = jnp.min(jnp.where(logits == m1, lane, LANES), axis=-1, keepdims=True)
    rest = jnp.where(lane == i1, -jnp.inf, logits)
    m2 = jnp.max(rest, axis=-1, keepdims=True)
    i2 = jnp.min(jnp.where(rest == m2, lane, LANES), axis=-1, keepdims=True)
    e21 = jnp.exp(m2 - m1)
    p1 = 1.0 / (1.0 + e21)
    p2 = e21 * p1
    oh1 = (lane == i1).astype(F32)
    oh2 = (lane == i2).astype(F32)
    oh = oh1 + oh2
    rows = h.shape[0]
    ri = lax.broadcasted_iota(jnp.int32, (rows, rows), 0)
    ci = lax.broadcasted_iota(jnp.int32, (rows, rows), 1)
    earlier = (ci < ri).astype(BF16)
    before = jnp.dot(earlier, oh.astype(BF16), preferred_element_type=F32) + run_scr[...]
    r1 = jnp.sum(oh1 * before, axis=-1, keepdims=True)
    r2 = jnp.sum(oh2 * before, axis=-1, keepdims=True)
    run = run_scr[...] + jnp.sum(oh, axis=0, keepdims=True)
    run_scr[...] = run
    cnt_ref[...] = run
    vals = (p1, p2, i1.astype(F32), i2.astype(F32), r1, r2)
    out = jnp.zeros_like(logits)
    for n, val in enumerate(vals):
        out = jnp.where(lane == n, val, out)
    route_ref[...] = out


def _router(x1, norm_g, sh, sc, rw_pad, rb_pad, seq):
    n, d = x1.shape
    tps = seq // TF
    mod_idx = lambda j: (j // tps, 0, 0)
    return pl.pallas_call(
        _router_kernel,
        grid=(n // TF,),
        in_specs=[pl.BlockSpec((TF, d), lambda j: (j, 0)),
                  pl.BlockSpec((1, d), lambda j: (0, 0)),
                  pl.BlockSpec((1, 1, d), mod_idx), pl.BlockSpec((1, 1, d), mod_idx),
                  pl.BlockSpec((d, LANES), lambda j: (0, 0)),
                  pl.BlockSpec((1, LANES), lambda j: (0, 0))],
        out_specs=[pl.BlockSpec((TF, d), lambda j: (j, 0)),
                   pl.BlockSpec((TF, LANES), lambda j: (j, 0)),
                   pl.BlockSpec((1, LANES), lambda j: (0, 0))],
        out_shape=[jax.ShapeDtypeStruct((n, d), F32),
                   jax.ShapeDtypeStruct((n, LANES), F32),
                   jax.ShapeDtypeStruct((1, LANES), F32)],
        scratch_shapes=[pltpu.VMEM((1, LANES), F32)],
        compiler_params=_params(1),
        name="router",
    )(x1, norm_g, sh, sc, rw_pad, rb_pad)


def _expert_kernel(be_ref, nb_ref, tok0_ref, tokn_ref, dst_ref, hx_ref, w1_ref, w3_ref, w2_ref, out_ref,
                   xbuf, ybuf, gsem, ssem, isem):
    del be_ref
    i = pl.program_id(0)
    nb = nb_ref[0]
    rows = MOE_ROWS
    slot = i % 2
    other = 1 - slot
    n_trash_blocks = N_EXPERTS
    trash0 = out_ref.shape[0] - (n_trash_blocks + 1) * rows

    def gather(tok_ref, s, lo, hi):
        for r in range(lo, hi):
            pltpu.make_async_copy(hx_ref.at[pl.ds(tok_ref[0, 0, r], 1)], xbuf.at[s, pl.ds(r, 1)],
                                  gsem.at[s]).start()

    def scatter(s, lo, hi):
        for r in range(lo, hi):
            pltpu.make_async_copy(ybuf.at[s, pl.ds(r, 1)], out_ref.at[pl.ds(dst_ref[0, 0, r], 1)],
                                  ssem.at[s]).start()

    def wait_gather(s):
        pltpu.make_async_copy(hx_ref.at[pl.ds(0, rows)], xbuf.at[s], gsem.at[s]).wait()

    def wait_scatter(s):
        pltpu.make_async_copy(ybuf.at[s], out_ref.at[pl.ds(0, rows)], ssem.at[s]).wait()

    @pl.when(i == 0)
    def _():
        ybuf[...] = jnp.zeros_like(ybuf)
        for t in range(n_trash_blocks):
            sem = ssem.at[0] if t == 0 else isem
            pltpu.make_async_copy(ybuf.at[0], out_ref.at[pl.ds(trash0 + t * rows, rows)], sem).start()
        for t in range(1, n_trash_blocks):
            pltpu.make_async_copy(ybuf.at[0], out_ref.at[pl.ds(trash0 + t * rows, rows)], isem).wait()
        gather(tok0_ref, 0, 0, rows)

    @pl.when(i < nb)
    def _():
        wait_gather(slot)
        hb = xbuf[slot].astype(BF16)
        w1, w3, w2 = w1_ref.at[0], w3_ref.at[0], w2_ref.at[0]
        d_ff = w1.shape[1]
        n_chunks = d_ff // FC
        per = -(-rows // n_chunks)
        acc = None
        for ci in range(n_chunks):
            f0 = ci * FC
            a = jnp.dot(hb, w1[:, f0:f0 + FC], preferred_element_type=F32)
            b = jnp.dot(hb, w3[:, f0:f0 + FC], preferred_element_type=F32)
            part = jnp.dot((_silu(a) * b).astype(BF16), w2[f0:f0 + FC, :], preferred_element_type=F32)
            acc = part if acc is None else acc + part
            lo, hi = ci * per, min(rows, (ci + 1) * per)
            gather(tokn_ref, other, lo, hi)
            scatter(other, lo, hi)
        wait_scatter(slot)
        ybuf[slot] = acc

    @pl.when(i == pl.num_programs(0) - 1)
    def _():
        last = (nb - 1) % 2
        for r in range(rows):
            pltpu.make_async_copy(ybuf.at[last, pl.ds(r, 1)], out_ref.at[pl.ds(dst_ref[0, 0, r], 1)],
                                  ssem.at[last]).start()
        wait_gather(1 - last)
        wait_scatter(0)
        wait_scatter(1)


def _expert_swiglu(hx, row_tok, row_dst, block_e, nb_used, w1, w3, w2, n_out_rows):
    n, d = hx.shape
    d_ff = w1.shape[2]
    n_blocks = row_tok.shape[0]
    w_idx = lambda i, be, nb: (be[i], 0, 0)
    expert = lambda shape: pl.BlockSpec(shape, w_idx, pipeline_mode=pl.Buffered(1))
    smem = lambda idx: pl.BlockSpec((1, 1, MOE_ROWS), idx, memory_space=pltpu.SMEM)
    return pl.pallas_call(
        _expert_kernel,
        grid_spec=pltpu.PrefetchScalarGridSpec(
            num_scalar_prefetch=2,
            grid=(n_blocks,),
            in_specs=[smem(lambda i, be, nb: (0, 0, 0)),
                      smem(lambda i, be, nb: (jnp.minimum(i + 1, nb[0] - 1), 0, 0)),
                      smem(lambda i, be, nb: (jnp.minimum(i, nb[0]), 0, 0)),
                      pl.BlockSpec(memory_space=pl.ANY),
                      expert((1, d, d_ff)), expert((1, d, d_ff)), expert((1, d_ff, d))],
            out_specs=pl.BlockSpec(memory_space=pl.ANY),
            scratch_shapes=[pltpu.VMEM((2, MOE_ROWS, d), F32), pltpu.VMEM((2, MOE_ROWS, d), F32),
                            pltpu.SemaphoreType.DMA((2,)), pltpu.SemaphoreType.DMA((2,)),
                            pltpu.SemaphoreType.DMA(())]),
        out_shape=jax.ShapeDtypeStruct((n_out_rows, d), F32),
        compiler_params=_params(1),
        name="expert_swiglu",
    )(block_e, nb_used, row_tok, row_tok, row_dst, hx, w1, w3, w2)


def _combine_kernel(x_ref, route_ref, g2_ref, fg_ref, y1_ref, y2_ref, o_ref):
    route = route_ref[...]
    y = route[:, 0:1] * y1_ref[...] + route[:, 1:2] * y2_ref[...]
    xn = x_ref[...] + g2_ref[0] * y
    o_ref[...] = xn * lax.rsqrt(jnp.mean(xn * xn, axis=-1, keepdims=True) + EPS) * fg_ref[...]


def _combine(x1, route, g2, final_g, y12, seq):
    n, d = x1.shape
    tps = seq // TF
    return pl.pallas_call(
        _combine_kernel,
        grid=(n // TF,),
        in_specs=[pl.BlockSpec((TF, d), lambda j: (j, 0)),
                  pl.BlockSpec((TF, LANES), lambda j: (j, 0)),
                  pl.BlockSpec((1, 1, d), lambda j: (j // tps, 0, 0)),
                  pl.BlockSpec((1, d), lambda j: (0, 0)),
                  pl.BlockSpec((TF, d), lambda j: (j, 0)),
                  pl.BlockSpec((TF, d), lambda j: (n // TF + j, 0))],
        out_specs=pl.BlockSpec((TF, d), lambda j: (j, 0)),
        out_shape=jax.ShapeDtypeStruct((n, d), F32),
        compiler_params=_params(1),
        name="combine",
    )(x1, route, g2, final_g, y12, y12)


def _moe_layer(x1, norm_g, sh, sc, g2, router_w, router_b, w1, w3, w2, final_g, seq):
    n, d = x1.shape
    rw_pad = jnp.zeros((d, LANES), F32).at[:, :N_EXPERTS].set(router_w)
    rb_pad = jnp.full((1, LANES), -1e30, F32).at[0, :N_EXPERTS].set(router_b)
    hx, route, counts = _router(x1, norm_g, sh, sc, rw_pad, rb_pad, seq)

    cnt = counts[0, :N_EXPERTS].astype(jnp.int32)
    padded = (cnt + MOE_ROWS - 1) // MOE_ROWS * MOE_ROWS
    seg_end = jnp.cumsum(padded)
    pad_start = seg_end - padded
    n_blocks = (2 * n) // MOE_ROWS + N_EXPERTS
    e12 = route[:, 2:4].astype(jnp.int32)
    r12 = route[:, 4:6].astype(jnp.int32)
    dest = pad_start[e12] + r12
    nb_used = (seg_end[-1] // MOE_ROWS).astype(jnp.int32).reshape(1)
    blk = jnp.minimum(jnp.arange(n_blocks, dtype=jnp.int32), nb_used[0] - 1) * MOE_ROWS
    block_e = jnp.minimum(jnp.searchsorted(seg_end, blk, side="right"), N_EXPERTS - 1).astype(jnp.int32)

    n_rows = n_blocks * MOE_ROWS
    r = jnp.arange(n_rows, dtype=jnp.int32)
    trash = 2 * n + block_e[r // MOE_ROWS] * MOE_ROWS + r % MOE_ROWS
    slot_major = jnp.arange(2, dtype=jnp.int32)[None, :] * n + jnp.arange(n, dtype=jnp.int32)[:, None]
    row_dst = trash.at[dest.reshape(-1)].set(slot_major.reshape(-1), unique_indices=True)
    row_tok = jnp.where(row_dst < 2 * n, row_dst % n, 0)
    dummy = 2 * n + N_EXPERTS * MOE_ROWS + jnp.arange(MOE_ROWS, dtype=jnp.int32)
    row_dst = jnp.concatenate([dummy, row_dst]).reshape(n_blocks + 1, 1, MOE_ROWS)
    n_out_rows = 2 * n + (N_EXPERTS + 1) * MOE_ROWS
    y12 = _expert_swiglu(hx, row_tok.reshape(n_blocks, 1, MOE_ROWS), row_dst, block_e, nb_used, w1, w3, w2,
                         n_out_rows)
    return _combine(x1, route, g2, final_g, y12, seq)


def _grid_pos_embed(rows, dim):
    row = jnp.broadcast_to(jnp.arange(rows, dtype=F32)[:, None], (rows, GRID_W)).reshape(-1)
    col = jnp.broadcast_to(jnp.arange(GRID_W, dtype=F32)[None, :], (rows, GRID_W)).reshape(-1)
    quarter = dim // 4
    omega = 1.0 / (10000.0 ** (jnp.arange(quarter, dtype=F32) / quarter))

    def enc(p):
        ang = p[:, None] * omega[None, :]
        return jnp.concatenate([jnp.sin(ang), jnp.cos(ang)], axis=-1)

    return jnp.concatenate([enc(row), enc(col)], axis=-1)


def _gate_lane_vec(t):
    v = jnp.zeros((2, 2 * GDN_HEADS), F32).at[:, :GDN_HEADS].set(t.astype(F32)).reshape(1, -1)
    return jnp.pad(v, ((0, 0), (0, GATE_PAD - v.shape[1])))


def kernel(x, c, ctx, c_ctx, w_mod, b_mod, norm1_g, norm2_g, w_in, conv_w, a_log, dt_bias, o_norm_g, sgu_norm_g, w_s, b_s, w_out, ffn_w1, ffn_w3, ffn_w2, router_w, router_b, exp_w1, exp_w3, exp_w2, final_g):
    bsz, seq, d = x.shape
    ctx_len = ctx.shape[1]
    depth = w_mod.shape[0]
    assert seq % TF == 0 and ctx_len % TM == 0 and (bsz * ctx_len) % TF == 0 and depth == 2
    n_lat = bsz * seq

    cond_rows = -(-(bsz + 1) // SUBLANES) * SUBLANES
    cond = jnp.zeros((cond_rows, d), F32).at[:bsz].set(c).at[bsz].set(c_ctx)
    mod = _modulation(cond, w_mod, b_mod)[:, :bsz + 1].reshape(depth, bsz + 1, 6, 1, d)

    pos_ext = jnp.concatenate([_grid_pos_embed(seq // GRID_W, d), jnp.zeros((TM, d), F32)], axis=0)
    xa = jnp.concatenate([x.reshape(n_lat, d), ctx.reshape(bsz * ctx_len, d)], axis=0)

    qkvz_w = 4 * GDN_WIDTH
    n_gate = 4 * GDN_HEADS
    out = None
    for layer in range(depth):
        last = layer == depth - 1
        sh1, sc1, g1, sh2, sc2, g2 = (mod[layer, :, i] for i in range(6))
        wl = w_in[layer]
        w_in_r = jnp.concatenate([wl[:, :qkvz_w], wl[:, qkvz_w + n_gate:], wl[:, qkvz_w:qkvz_w + n_gate],
                                  jnp.zeros((d, GATE_PAD - n_gate), F32)], axis=1).astype(BF16)
        b_s_b = jnp.broadcast_to(b_s[layer][:, :, None], (GMLP_GROUPS, GMLP_CHUNK, GMLP_GROUP_DIM))
        pos = pos_ext if layer == 0 else None
        qkv, zs, gates, mx = _in_stage(
            xa, pos, norm1_g[layer][None], sh1, sc1, w_in_r, conv_w[layer],
            _gate_lane_vec(a_log[layer]), _gate_lane_vec(dt_bias[layer]), sgu_norm_g[layer][None],
            w_s[layer].astype(BF16), b_s_b, bsz, seq, ctx_len)
        o_f, o_b = _gdn_scan(qkv, gates, bsz, seq, ctx_len)
        n_rows = n_lat if last else xa.shape[0]
        xa = _out_stage(xa, pos, o_f, o_b, zs, mx, o_norm_g[layer][None], g1, w_out[layer].astype(BF16),
                        n_rows, bsz, seq, ctx_len)
        i = layer // 2
        if layer % 2 == 0:
            xa = _dense_ffn(xa, norm2_g[layer][None], sh2, sc2, g2, ffn_w1[i].astype(BF16),
                            ffn_w3[i].astype(BF16), ffn_w2[i].astype(BF16), bsz, seq)
        else:
            out = _moe_layer(xa, norm2_g[layer][None], sh2, sc2, g2, router_w[i], router_b[i],
                             exp_w1[i].astype(BF16), exp_w3[i].astype(BF16), exp_w2[i].astype(BF16),
                             final_g[None], seq)
    return out.reshape(bsz, seq, d)
```

```python
import functools

import jax
import jax.numpy as jnp
from jax import lax
from jax.experimental import pallas as pl
from jax.experimental.pallas import tpu as pltpu

F32 = jnp.float32
BF16 = jnp.bfloat16
EPS = 1e-6
HIGHEST = lax.Precision.HIGHEST

HEAD_DIM = 128
GDN_HEADS = 4
GDN_WIDTH = GDN_HEADS * HEAD_DIM
GDN_CHUNK = 64
CONV_K = 5
GMLP_GROUPS = 4
GMLP_GROUP_DIM = 128
GMLP_WIDTH = GMLP_GROUPS * GMLP_GROUP_DIM
GMLP_CHUNK = 128
GRID_W = 64
N_EXPERTS = 8

LANES = 128
SUBLANES = 8
VMEM_LIMIT = 56 * 1024 * 1024

TM = 256
HALO = SUBLANES
TF = 512
MOE_ROWS = 512
FC = 512
GATE_PAD = LANES


def _sigmoid(x):
    return 1.0 / (1.0 + jnp.exp(-x))


def _silu(x):
    return x * _sigmoid(x)


def _gelu_tanh(x):
    c = 0.7978845608028654
    return 0.5 * x * (1.0 + jnp.tanh(c * (x + 0.044715 * (x * x * x))))


def _softplus(x):
    return jnp.maximum(x, 0.0) + jnp.log1p(jnp.exp(-jnp.abs(x)))


def _rms_mod(x, g, shift, scale):
    ms = jnp.mean(x * x, axis=-1, keepdims=True)
    return (x * lax.rsqrt(ms + EPS) * g) * (1.0 + scale) + shift


def _params(n_axes=1):
    return pltpu.CompilerParams(dimension_semantics=("arbitrary",) * n_axes,
                                vmem_limit_bytes=VMEM_LIMIT)


def _mod_kernel(c_ref, w_ref, b_ref, o_ref):
    s = _silu(c_ref[...])
    o_ref[0] = jnp.dot(s, w_ref[0], precision=HIGHEST, preferred_element_type=F32) + b_ref[0]


def _modulation(cond, w_mod, b_mod):
    depth, d, six_d = w_mod.shape
    rows = cond.shape[0]
    tn = 1536
    return pl.pallas_call(
        _mod_kernel,
        grid=(depth, six_d // tn),
        in_specs=[pl.BlockSpec((rows, d), lambda l, n: (0, 0)),
                  pl.BlockSpec((1, d, tn), lambda l, n: (l, 0, n)),
                  pl.BlockSpec((1, 1, tn), lambda l, n: (l, 0, n))],
        out_specs=pl.BlockSpec((1, rows, tn), lambda l, n: (l, 0, n)),
        out_shape=jax.ShapeDtypeStruct((depth, rows, six_d), F32),
        compiler_params=_params(2),
        name="modulation",
    )(cond, w_mod, b_mod.reshape(depth, 1, six_d))


def _in_stage_kernel(add_pos, nx_tiles, tps, ctps, *refs):
    if add_pos:
        (x_ref, xp_ref, xn_ref, pos_ref, posp_ref, posn_ref, g_ref, sh_ref, sc_ref, w_ref, cw_ref,
         alog_ref, dt_ref, sg_ref, ws_ref, bs_ref, qkv_ref, z_ref, gate_ref, mx_ref, pe_scr) = refs
    else:
        (x_ref, xp_ref, xn_ref, g_ref, sh_ref, sc_ref, w_ref, cw_ref,
         alog_ref, dt_ref, sg_ref, ws_ref, bs_ref, qkv_ref, z_ref, gate_ref, mx_ref, pe_scr) = refs
    j = pl.program_id(0)
    is_x = j < nx_tiles
    tis = jnp.where(is_x, j % tps, (j - nx_tiles) % ctps)
    n_in_seq = jnp.where(is_x, tps, ctps)
    first = tis == 0
    last = tis == n_in_seq - 1

    xm, xp, xn = x_ref[...], xp_ref[...], xn_ref[...]
    if add_pos:
        xm, xp, xn = xm + pos_ref[...], xp + posp_ref[...], xn + posn_ref[...]
    xe = jnp.concatenate([xp, xm, xn], axis=0)
    h = _rms_mod(xe, g_ref[...], sh_ref[0], sc_ref[0])
    p = jnp.dot(h.astype(BF16), w_ref[...], preferred_element_type=F32)

    qkv_w = 3 * GDN_WIDTH
    rows = lax.broadcasted_iota(jnp.int32, (TM + 2 * HALO, 1), 0)
    valid = jnp.logical_and(jnp.logical_or(rows >= HALO, jnp.logical_not(first)),
                            jnp.logical_or(rows < TM + HALO, jnp.logical_not(last)))
    pe_scr[...] = jnp.where(valid, p[:, :qkv_w], 0.0)
    acc = cw_ref[0:1, :] * pe_scr[pl.ds(HALO - CONV_K // 2, TM), :]
    for t in range(1, CONV_K):
        acc = acc + cw_ref[t:t + 1, :] * pe_scr[pl.ds(HALO - CONV_K // 2 + t, TM), :]
    qkv = _silu(acc)
    for i in range(3 * GDN_HEADS):
        blk = qkv[:, i * HEAD_DIM:(i + 1) * HEAD_DIM]
        if i < 2 * GDN_HEADS:
            blk = blk * lax.rsqrt(jnp.sum(blk * blk, axis=-1, keepdims=True) + EPS)
        if i < GDN_HEADS:
            blk = blk * (HEAD_DIM ** -0.5)
        qkv_ref[:, i * HEAD_DIM:(i + 1) * HEAD_DIM] = blk

    pm = p[HALO:HALO + TM, :]
    z_ref[...] = _silu(pm[:, qkv_w:qkv_w + GDN_WIDTH])

    c0 = qkv_w + GDN_WIDTH
    gu = _gelu_tanh(pm[:, c0:c0 + GMLP_WIDTH])
    gvr = _gelu_tanh(pm[:, c0 + GMLP_WIDTH:c0 + 2 * GMLP_WIDTH])
    mu = jnp.mean(gvr, axis=-1, keepdims=True)
    xc = gvr - mu
    gv = xc * lax.rsqrt(jnp.mean(xc * xc, axis=-1, keepdims=True) + EPS) * sg_ref[...]
    for r in range(TM // GMLP_CHUNK):
        rs = slice(r * GMLP_CHUNK, (r + 1) * GMLP_CHUNK)
        for g in range(GMLP_GROUPS):
            cs = slice(g * GMLP_GROUP_DIM, (g + 1) * GMLP_GROUP_DIM)
            mixed = jnp.dot(ws_ref[g], gv[rs, cs].astype(BF16), preferred_element_type=F32) + bs_ref[g]
            mx_ref[rs, cs] = (gu[rs, cs] * mixed).astype(BF16)

    graw = pm[:, c0 + 2 * GMLP_WIDTH:]
    lane = lax.broadcasted_iota(jnp.int32, graw.shape, 1)
    is_decay = (lane % (2 * GDN_HEADS)) < GDN_HEADS
    decay = -jnp.exp(alog_ref[...]) * _softplus(graw + dt_ref[...])
    gate_ref[...] = jnp.where(is_decay, decay, _sigmoid(graw))


def _in_stage(xa, pos_ext, norm_g, sh, sc, w_in_r, conv_w, alog_vec, dt_vec, sgu_g, w_s, b_s_b,
              bsz, seq, ctx_len):
    t_all, d = xa.shape
    nx_tiles = bsz * seq // TM
    tps, ctps = seq // TM, ctx_len // TM
    n_tiles = t_all // TM
    hb = TM // HALO
    add_pos = pos_ext is not None
    pw = w_in_r.shape[1]

    def mod_idx(j):
        return (jnp.where(j < nx_tiles, j // tps, bsz), 0, 0)

    in_specs = [pl.BlockSpec((TM, d), lambda j: (j, 0)),
                pl.BlockSpec((HALO, d), lambda j: (jnp.maximum(j * hb - 1, 0), 0)),
                pl.BlockSpec((HALO, d), lambda j: (jnp.minimum((j + 1) * hb, t_all // HALO - 1), 0))]
    args = [xa, xa, xa]
    if add_pos:
        in_specs += [
            pl.BlockSpec((TM, d), lambda j: (jnp.where(j < nx_tiles, j % tps, tps), 0)),
            pl.BlockSpec((HALO, d), lambda j: (jnp.where(j < nx_tiles, jnp.maximum((j % tps) * hb - 1, 0),
                                                         seq // HALO), 0)),
            pl.BlockSpec((HALO, d), lambda j: (jnp.where(j < nx_tiles, (j % tps + 1) * hb, seq // HALO), 0))]
        args += [pos_ext, pos_ext, pos_ext]
    const2 = lambda j: (0, 0)
    const3 = lambda j: (0, 0, 0)
    in_specs += [pl.BlockSpec((1, d), const2),
                 pl.BlockSpec((1, 1, d), mod_idx), pl.BlockSpec((1, 1, d), mod_idx),
                 pl.BlockSpec((d, pw), const2),
                 pl.BlockSpec((CONV_K, 3 * GDN_WIDTH), const2),
                 pl.BlockSpec((1, GATE_PAD), const2), pl.BlockSpec((1, GATE_PAD), const2),
                 pl.BlockSpec((1, GMLP_WIDTH), const2),
                 pl.BlockSpec((GMLP_GROUPS, GMLP_CHUNK, GMLP_CHUNK), const3),
                 pl.BlockSpec((GMLP_GROUPS, GMLP_CHUNK, GMLP_GROUP_DIM), const3)]
    args += [norm_g, sh, sc, w_in_r, conv_w, alog_vec, dt_vec, sgu_g, w_s, b_s_b]
    widths = (3 * GDN_WIDTH, GDN_WIDTH, GATE_PAD, GMLP_WIDTH)
    return pl.pallas_call(
        functools.partial(_in_stage_kernel, add_pos, nx_tiles, tps, ctps),
        grid=(n_tiles,),
        in_specs=in_specs,
        out_specs=[pl.BlockSpec((TM, w), lambda j: (j, 0)) for w in widths],
        out_shape=[jax.ShapeDtypeStruct((t_all, w), dt) for w, dt in zip(widths, (F32, F32, F32, BF16))],
        scratch_shapes=[pltpu.VMEM((TM + 2 * HALO, 3 * GDN_WIDTH), F32)],
        compiler_params=_params(1),
        name="in_stage",
    )(*args)


INV_BASE = 16


GDN_BATCHES = 4


def _dot_bf16(a, b, dims=None):
    a, b = a.astype(BF16), b.astype(BF16)
    if dims is None:
        return jnp.dot(a, b, preferred_element_type=F32)
    return lax.dot_general(a, b, (dims, ((), ())), preferred_element_type=F32)


def _gdn_kernel(n_b, *refs):
    ins, outs, s_ref = refs[:4 * n_b], refs[4 * n_b:4 * n_b + 2], refs[4 * n_b + 2]

    @pl.when(pl.program_id(1) == 0)
    def _():
        s_ref[...] = jnp.zeros_like(s_ref)

    c = GDN_CHUNK
    ii = lax.broadcasted_iota(jnp.int32, (c, c), 0)
    jj = lax.broadcasted_iota(jnp.int32, (c, c), 1)
    eye = ii == jj
    eye_f = eye.astype(F32)
    bi, bj = ii // INV_BASE, jj // INV_BASE
    diag_blk = bi == bj
    merge_masks = []
    size = INV_BASE
    while size < c:
        grp = 2 * size // INV_BASE
        merge_masks.append(jnp.logical_and(bi // grp == bj // grp, bi // (grp // 2) != bj // (grp // 2)))
        size *= 2

    chains = []
    for g in range(n_b):
        for reverse in (False, True):
            qkv_ref, gate_ref = ins[4 * g + 2 * reverse], ins[4 * g + 2 * reverse + 1]
            incl, strict = (jj >= ii, jj > ii) if reverse else (jj <= ii, jj < ii)
            lane0 = 2 * GDN_HEADS * reverse
            gates = gate_ref[...]
            gc_all = jnp.dot(incl.astype(F32), gates, precision=HIGHEST, preferred_element_type=F32)
            tot_all = gc_all[0:1, :] if reverse else gc_all[c - 1:c, :]
            for h in range(GDN_HEADS):
                hs = slice(h * HEAD_DIM, (h + 1) * HEAD_DIM)
                chains.append(dict(
                    qkv=qkv_ref, h=h, hs=hs, g=g, o_ref=outs[reverse], incl=incl, strict=strict,
                    state=(2 * g + reverse) * GDN_HEADS + h,
                    gc=gc_all[:, lane0 + h:lane0 + h + 1],
                    beta=gates[:, lane0 + GDN_HEADS + h:lane0 + GDN_HEADS + h + 1],
                    tot=tot_all[:, lane0 + h:lane0 + h + 1]))

    def head(ch, part):
        return ch["qkv"][:, part * GDN_WIDTH + ch["h"] * HEAD_DIM:part * GDN_WIDTH + (ch["h"] + 1) * HEAD_DIM]

    for ch in chains:
        gcb = jnp.broadcast_to(ch["gc"], (c, c))
        gc_row = jnp.sum(jnp.where(eye, gcb, 0.0), axis=0, keepdims=True)
        ch["gamma"] = jnp.where(ch["incl"], jnp.exp(jnp.where(ch["incl"], gcb - gc_row, 0.0)), 0.0)
        ch["egc"] = jnp.exp(ch["gc"])
        q, k = head(ch, 0), head(ch, 1)
        ch["kb"] = k * ch["beta"]
        ch["kq"] = _dot_bf16(jnp.concatenate([ch["kb"], q], axis=0), k, ((1,), (1,)))
    for ch in chains:
        ch["a"] = jnp.where(ch["strict"], ch["kq"][:c] * ch["gamma"], 0.0)
        ch["qk"] = jnp.where(ch["incl"], ch["kq"][c:] * ch["gamma"], 0.0).astype(BF16)
        x = jnp.where(diag_blk, -ch["a"], 0.0)
        ch["inv"] = eye_f + x
        ch["pw"] = _dot_bf16(x, x)
    n_sq = INV_BASE.bit_length() - 1
    for step in range(1, n_sq):
        for ch in chains:
            if step < n_sq - 1:
                r = _dot_bf16(jnp.concatenate([ch["inv"], ch["pw"]], axis=0), ch["pw"])
                ch["inv"], ch["pw"] = ch["inv"] + r[:c], r[c:]
            else:
                ch["inv"] = ch["inv"] + _dot_bf16(ch["inv"], ch["pw"])
    for off in merge_masks:
        for ch in chains:
            ch["tmp"] = _dot_bf16(ch["inv"], jnp.where(off, ch["a"], 0.0))
        for ch in chains:
            ch["inv"] = ch["inv"] - _dot_bf16(ch["tmp"], ch["inv"])
    for ch in chains:
        rhs = jnp.concatenate([head(ch, 2) * ch["beta"], ch["kb"] * ch["egc"]], axis=1)
        ch["uw"] = _dot_bf16(ch["inv"], rhs)
    for ch in chains:
        ch["s"] = s_ref[ch["state"]]
        lhs = jnp.concatenate([ch["uw"][:, HEAD_DIM:], head(ch, 0) * ch["egc"]], axis=0)
        ch["wq"] = _dot_bf16(lhs, ch["s"])
    for ch in chains:
        vb = (ch["uw"][:, :HEAD_DIM] - ch["wq"][:c]).astype(BF16)
        ch["o_ref"][ch["g"], :, ch["hs"]] = ch["wq"][c:] + jnp.dot(ch["qk"], vb, preferred_element_type=F32)
        k_dec = head(ch, 1) * jnp.exp(ch["tot"] - ch["gc"])
        s_ref[ch["state"]] = ch["s"] * jnp.exp(ch["tot"]) + _dot_bf16(k_dec, vb, ((0,), (0,)))


def _gdn_scan(qkv, gates, bsz, seq, ctx_len):
    t_all = qkv.shape[0]
    c = GDN_CHUNK
    n_ctx, n_x = ctx_len // c, seq // c
    n_steps = n_ctx + n_x
    ctx_base = bsz * n_x
    n_b = GDN_BATCHES if bsz % GDN_BATCHES == 0 else 1
    per = bsz // n_b

    def fwd_idx(g):
        def idx(bg, s):
            b = g * per + bg
            return (jnp.where(s < n_ctx, ctx_base + b * n_ctx + s, b * n_x + s - n_ctx), 0)
        return idx

    def bwd_idx(g):
        def idx(bg, s):
            b = g * per + bg
            return (jnp.where(s < n_ctx, ctx_base + b * n_ctx + n_ctx - 1 - s, b * n_x + n_steps - 1 - s), 0)
        return idx

    qw = 3 * GDN_WIDTH
    in_specs = []
    for g in range(n_b):
        in_specs += [pl.BlockSpec((c, qw), fwd_idx(g)), pl.BlockSpec((c, GATE_PAD), fwd_idx(g)),
                     pl.BlockSpec((c, qw), bwd_idx(g)), pl.BlockSpec((c, GATE_PAD), bwd_idx(g))]
    plane_ctx_base = per * n_x
    out_f = lambda bg, s: (0, jnp.where(s < n_ctx, plane_ctx_base + bg * n_ctx + s, bg * n_x + s - n_ctx), 0)
    out_b = lambda bg, s: (0, jnp.where(s < n_ctx, plane_ctx_base + bg * n_ctx + n_ctx - 1 - s,
                                        bg * n_x + n_steps - 1 - s), 0)
    return pl.pallas_call(
        functools.partial(_gdn_kernel, n_b),
        grid=(bsz // n_b, n_steps),
        in_specs=in_specs,
        out_specs=[pl.BlockSpec((n_b, c, GDN_WIDTH), out_f), pl.BlockSpec((n_b, c, GDN_WIDTH), out_b)],
        out_shape=[jax.ShapeDtypeStruct((n_b, t_all // n_b, GDN_WIDTH), F32)] * 2,
        scratch_shapes=[pltpu.VMEM((2 * n_b * GDN_HEADS, HEAD_DIM, HEAD_DIM), F32)],
        compiler_params=_params(2),
        name="gdn_scan",
    )(*([qkv, gates, qkv, gates] * n_b))


def _mix_out(add_pos, refs):
    if add_pos:
        x_ref, pos_ref, of_ref, ob_ref, z_ref, mx_ref, og_ref, g1_ref, w_ref = refs
    else:
        x_ref, of_ref, ob_ref, z_ref, mx_ref, og_ref, g1_ref, w_ref = refs
    o = of_ref[...] + ob_ref[...]
    zs = z_ref[...]
    parts = []
    for h in range(GDN_HEADS):
        hs = slice(h * HEAD_DIM, (h + 1) * HEAD_DIM)
        oh = o[:, hs]
        oh = oh * lax.rsqrt(jnp.mean(oh * oh, axis=-1, keepdims=True) + EPS) * og_ref[...]
        parts.append((oh * zs[:, hs]).astype(BF16))
    parts.append(mx_ref[...])
    y = jnp.dot(jnp.concatenate(parts, axis=-1), w_ref[...], preferred_element_type=F32)
    x = x_ref[...]
    if add_pos:
        x = x + pos_ref[...]
    return x + g1_ref[0] * y


def _mix_inputs(xa, pos_ext, o_f, o_b, zs, mx, o_norm_g, g1, w_out, bsz, seq, ctx_len):
    d = xa.shape[1]
    nx_tiles, tps = bsz * seq // TF, seq // TF
    n_b = o_f.shape[0]
    per = bsz // n_b
    seqs_per_tile = max(TF // ctx_len, 1)
    assert (TF % ctx_len == 0 or ctx_len % TF == 0) and per % seqs_per_tile == 0
    row = lambda w: pl.BlockSpec((TF, w), lambda j: (j, 0))

    def scan_idx(j):
        off = (j - nx_tiles) * TF
        b = jnp.where(j < nx_tiles, j // tps, off // ctx_len)
        tile = jnp.where(j < nx_tiles, (b % per) * tps + j % tps,
                         (per * seq + (b % per) * ctx_len + off % ctx_len) // TF)
        return (b // per, tile, 0)

    scan = pl.BlockSpec((None, TF, GDN_WIDTH), scan_idx)
    in_specs, args = [row(d)], [xa]
    if pos_ext is not None:
        in_specs.append(pl.BlockSpec((TF, d), lambda j: (jnp.where(j < nx_tiles, j % tps, tps), 0)))
        args.append(pos_ext)
    in_specs += [scan, scan, row(GDN_WIDTH), row(GMLP_WIDTH),
                 pl.BlockSpec((1, HEAD_DIM), lambda j: (0, 0)),
                 pl.BlockSpec((1, 1, d), lambda j: (jnp.where(j < nx_tiles, j // tps, bsz), 0, 0)),
                 pl.BlockSpec((GDN_WIDTH + GMLP_WIDTH, d), lambda j: (0, 0))]
    args += [o_f, o_b, zs, mx, o_norm_g, g1, w_out]
    return in_specs, args


def _swiglu_body(hb, w1, w3, w2):
    d_ff = w1.shape[1]
    acc = None
    for f0 in range(0, d_ff, FC):
        a = jnp.dot(hb, w1[:, f0:f0 + FC], preferred_element_type=F32)
        b = jnp.dot(hb, w3[:, f0:f0 + FC], preferred_element_type=F32)
        part = jnp.dot((_silu(a) * b).astype(BF16), w2[f0:f0 + FC, :], preferred_element_type=F32)
        acc = part if acc is None else acc + part
    return acc


def _mix_ffn_kernel(add_pos, n_mix, *refs):
    g_ref, sh_ref, sc_ref, g2_ref, w1_ref, w3_ref, w2_ref, o_ref = refs[n_mix:]
    x = _mix_out(add_pos, refs[:n_mix])
    h = _rms_mod(x, g_ref[...], sh_ref[0], sc_ref[0])
    o_ref[...] = x + g2_ref[0] * _swiglu_body(h.astype(BF16), w1_ref, w3_ref, w2_ref)


def _mix_dense_ffn(mix_specs, mix_args, add_pos, norm_g, sh, sc, g2, w1, w3, w2, bsz, seq):
    t_all, d = mix_args[0].shape
    d_ff = w1.shape[1]
    nx_tiles, tps = bsz * seq // TF, seq // TF
    mod_idx = lambda j: (jnp.where(j < nx_tiles, j // tps, bsz), 0, 0)
    resident = lambda shape: pl.BlockSpec(shape, lambda j: (0, 0), pipeline_mode=pl.Buffered(1))
    return pl.pallas_call(
        functools.partial(_mix_ffn_kernel, add_pos, len(mix_args)),
        grid=(t_all // TF,),
        in_specs=mix_specs + [pl.BlockSpec((1, d), lambda j: (0, 0)),
                              pl.BlockSpec((1, 1, d), mod_idx), pl.BlockSpec((1, 1, d), mod_idx),
                              pl.BlockSpec((1, 1, d), mod_idx),
                              resident((d, d_ff)), resident((d, d_ff)), resident((d_ff, d))],
        out_specs=pl.BlockSpec((TF, d), lambda j: (j, 0)),
        out_shape=jax.ShapeDtypeStruct((t_all, d), F32),
        compiler_params=_params(1),
        name="mix_dense_ffn",
    )(*mix_args, norm_g, sh, sc, g2, w1, w3, w2)


def _mix_router_kernel(n_mix, *refs):
    g_ref, sh_ref, sc_ref, rw_ref, rb_ref, x1_ref, h_ref, route_ref, cnt_ref, run_scr = refs[n_mix:]
    i = pl.program_id(0)

    @pl.when(i == 0)
    def _():
        run_scr[...] = jnp.zeros_like(run_scr)

    x1 = _mix_out(False, refs[:n_mix])
    x1_ref[...] = x1
    h = _rms_mod(x1, g_ref[...], sh_ref[0], sc_ref[0])
    h_ref[...] = h
    logits = jnp.dot(h, rw_ref[...], precision=HIGHEST, preferred_element_type=F32) + rb_ref[...]
    lane = lax.broadcasted_iota(jnp.int32, logits.shape, 1)
    m1 = jnp.max(logits, axis=-1, keepdims=True)
    i1 = jnp.min(jnp.where(logits == m1, lane, LANES), axis=-1, keepdims=True)
    rest = jnp.where(lane == i1, -jnp.inf, logits)
    m2 = jnp.max(rest, axis=-1, keepdims=True)
    i2 = jnp.min(jnp.where(rest == m2, lane, LANES), axis=-1, keepdims=True)
    e21 = jnp.exp(m2 - m1)
    p1 = 1.0 / (1.0 + e21)
    p2 = e21 * p1
    oh1 = (lane == i1).astype(F32)
    oh2 = (lane == i2).astype(F32)
    oh = oh1 + oh2
    rows = h.shape[0]
    ri = lax.broadcasted_iota(jnp.int32, (rows, rows), 0)
    ci = lax.broadcasted_iota(jnp.int32, (rows, rows), 1)
    earlier = (ci < ri).astype(BF16)
    before = jnp.dot(earlier, oh.astype(BF16), preferred_element_type=F32) + run_scr[...]
    r1 = jnp.sum(oh1 * before, axis=-1, keepdims=True)
    r2 = jnp.sum(oh2 * before, axis=-1, keepdims=True)
    run = run_scr[...] + jnp.sum(oh, axis=0, keepdims=True)
    run_scr[...] = run
    cnt_ref[...] = run
    vals = (p1, p2, i1.astype(F32), i2.astype(F32), r1, r2)
    out = jnp.zeros_like(logits)
    for n, val in enumerate(vals):
        out = jnp.where(lane == n, val, out)
    route_ref[...] = out


def _mix_router(mix_specs, mix_args, norm_g, sh, sc, rw_pad, rb_pad, n, seq):
    d = mix_args[0].shape[1]
    tps = seq // TF
    mod_idx = lambda j: (j // tps, 0, 0)
    row = lambda w: pl.BlockSpec((TF, w), lambda j: (j, 0))
    return pl.pallas_call(
        functools.partial(_mix_router_kernel, len(mix_args)),
        grid=(n // TF,),
        in_specs=mix_specs + [pl.BlockSpec((1, d), lambda j: (0, 0)),
                              pl.BlockSpec((1, 1, d), mod_idx), pl.BlockSpec((1, 1, d), mod_idx),
                              pl.BlockSpec((d, LANES), lambda j: (0, 0)),
                              pl.BlockSpec((1, LANES), lambda j: (0, 0))],
        out_specs=[row(d), row(d), row(LANES), pl.BlockSpec((1, LANES), lambda j: (0, 0))],
        out_shape=[jax.ShapeDtypeStruct((n, d), F32), jax.ShapeDtypeStruct((n, d), F32),
                   jax.ShapeDtypeStruct((n, LANES), F32), jax.ShapeDtypeStruct((1, LANES), F32)],
        scratch_shapes=[pltpu.VMEM((1, LANES), F32)],
        compiler_params=_params(1),
        name="mix_router",
    )(*mix_args, norm_g, sh, sc, rw_pad, rb_pad)


def _expert_kernel(be_ref, nb_ref, tok0_ref, tokn_ref, dst_ref, hx_ref, w1_ref, w3_ref, w2_ref, out_ref,
                   xbuf, ybuf, gsem, ssem, isem):
    del be_ref
    i = pl.program_id(0)
    nb = nb_ref[0]
    rows = MOE_ROWS
    slot = i % 2
    other = 1 - slot
    n_trash_blocks = N_EXPERTS
    trash0 = out_ref.shape[0] - (n_trash_blocks + 1) * rows

    def gather(tok_ref, s, lo, hi):
        for r in range(lo, hi):
            pltpu.make_async_copy(hx_ref.at[pl.ds(tok_ref[0, 0, r], 1)], xbuf.at[s, pl.ds(r, 1)],
                                  gsem.at[s]).start()

    def scatter(s, lo, hi):
        for r in range(lo, hi):
            pltpu.make_async_copy(ybuf.at[s, pl.ds(r, 1)], out_ref.at[pl.ds(dst_ref[0, 0, r], 1)],
                                  ssem.at[s]).start()

    def wait_gather(s):
        pltpu.make_async_copy(hx_ref.at[pl.ds(0, rows)], xbuf.at[s], gsem.at[s]).wait()

    def wait_scatter(s):
        pltpu.make_async_copy(ybuf.at[s], out_ref.at[pl.ds(0, rows)], ssem.at[s]).wait()

    @pl.when(i == 0)
    def _():
        ybuf[...] = jnp.zeros_like(ybuf)
        for t in range(n_trash_blocks):
            sem = ssem.at[0] if t == 0 else isem
            pltpu.make_async_copy(ybuf.at[0], out_ref.at[pl.ds(trash0 + t * rows, rows)], sem).start()
        for t in range(1, n_trash_blocks):
            pltpu.make_async_copy(ybuf.at[0], out_ref.at[pl.ds(trash0 + t * rows, rows)], isem).wait()
        gather(tok0_ref, 0, 0, rows)

    @pl.when(i < nb)
    def _():
        wait_gather(slot)
        hb = xbuf[slot].astype(BF16)
        w1, w3, w2 = w1_ref.at[0], w3_ref.at[0], w2_ref.at[0]
        d_ff = w1.shape[1]
        n_chunks = d_ff // FC
        per = -(-rows // n_chunks)
        acc = None
        for ci in range(n_chunks):
            f0 = ci * FC
            a = jnp.dot(hb, w1[:, f0:f0 + FC], preferred_element_type=F32)
            b = jnp.dot(hb, w3[:, f0:f0 + FC], preferred_element_type=F32)
            part = jnp.dot((_silu(a) * b).astype(BF16), w2[f0:f0 + FC, :], preferred_element_type=F32)
            acc = part if acc is None else acc + part
            lo, hi = ci * per, min(rows, (ci + 1) * per)
            gather(tokn_ref, other, lo, hi)
            scatter(other, lo, hi)
        wait_scatter(slot)
        ybuf[slot] = acc

    @pl.when(i == pl.num_programs(0) - 1)
    def _():
        last = (nb - 1) % 2
        for r in range(rows):
            pltpu.make_async_copy(ybuf.at[last, pl.ds(r, 1)], out_ref.at[pl.ds(dst_ref[0, 0, r], 1)],
                                  ssem.at[last]).start()
        wait_gather(1 - last)
        wait_scatter(0)
        wait_scatter(1)


def _expert_swiglu(hx, row_tok, row_dst, block_e, nb_used, w1, w3, w2, n_out_rows):
    n, d = hx.shape
    d_ff = w1.shape[2]
    n_blocks = row_tok.shape[0]
    w_idx = lambda i, be, nb: (be[i], 0, 0)
    expert = lambda shape: pl.BlockSpec(shape, w_idx, pipeline_mode=pl.Buffered(1))
    smem = lambda idx: pl.BlockSpec((1, 1, MOE_ROWS), idx, memory_space=pltpu.SMEM)
    return pl.pallas_call(
        _expert_kernel,
        grid_spec=pltpu.PrefetchScalarGridSpec(
            num_scalar_prefetch=2,
            grid=(n_blocks,),
            in_specs=[smem(lambda i, be, nb: (0, 0, 0)),
                      smem(lambda i, be, nb: (jnp.minimum(i + 1, nb[0] - 1), 0, 0)),
                      smem(lambda i, be, nb: (jnp.minimum(i, nb[0]), 0, 0)),
                      pl.BlockSpec(memory_space=pl.ANY),
                      expert((1, d, d_ff)), expert((1, d, d_ff)), expert((1, d_ff, d))],
            out_specs=pl.BlockSpec(memory_space=pl.ANY),
            scratch_shapes=[pltpu.VMEM((2, MOE_ROWS, d), F32), pltpu.VMEM((2, MOE_ROWS, d), F32),
                            pltpu.SemaphoreType.DMA((2,)), pltpu.SemaphoreType.DMA((2,)),
                            pltpu.SemaphoreType.DMA(())]),
        out_shape=jax.ShapeDtypeStruct((n_out_rows, d), F32),
        compiler_params=_params(1),
        name="expert_swiglu",
    )(block_e, nb_used, row_tok, row_tok, row_dst, hx, w1, w3, w2)


def _combine_kernel(x_ref, route_ref, g2_ref, fg_ref, y1_ref, y2_ref, o_ref):
    route = route_ref[...]
    y = route[:, 0:1] * y1_ref[...] + route[:, 1:2] * y2_ref[...]
    xn = x_ref[...] + g2_ref[0] * y
    o_ref[...] = xn * lax.rsqrt(jnp.mean(xn * xn, axis=-1, keepdims=True) + EPS) * fg_ref[...]


def _combine(x1, route, g2, final_g, y12, seq):
    n, d = x1.shape
    tps = seq // TF
    return pl.pallas_call(
        _combine_kernel,
        grid=(n // TF,),
        in_specs=[pl.BlockSpec((TF, d), lambda j: (j, 0)),
                  pl.BlockSpec((TF, LANES), lambda j: (j, 0)),
                  pl.BlockSpec((1, 1, d), lambda j: (j // tps, 0, 0)),
                  pl.BlockSpec((1, d), lambda j: (0, 0)),
                  pl.BlockSpec((TF, d), lambda j: (j, 0)),
                  pl.BlockSpec((TF, d), lambda j: (n // TF + j, 0))],
        out_specs=pl.BlockSpec((TF, d), lambda j: (j, 0)),
        out_shape=jax.ShapeDtypeStruct((n, d), F32),
        compiler_params=_params(1),
        name="combine",
    )(x1, route, g2, final_g, y12, y12)


def _moe_layer(mix_specs, mix_args, n, norm_g, sh, sc, g2, router_w, router_b, w1, w3, w2, final_g, seq):
    d = mix_args[0].shape[1]
    rw_pad = jnp.zeros((d, LANES), F32).at[:, :N_EXPERTS].set(router_w)
    rb_pad = jnp.full((1, LANES), -1e30, F32).at[0, :N_EXPERTS].set(router_b)
    x1, hx, route, counts = _mix_router(mix_specs, mix_args, norm_g, sh, sc, rw_pad, rb_pad, n, seq)

    cnt = counts[0, :N_EXPERTS].astype(jnp.int32)
    padded = (cnt + MOE_ROWS - 1) // MOE_ROWS * MOE_ROWS
    seg_end = jnp.cumsum(padded)
    pad_start = seg_end - padded
    n_blocks = (2 * n) // MOE_ROWS + N_EXPERTS
    e12 = route[:, 2:4].astype(jnp.int32)
    r12 = route[:, 4:6].astype(jnp.int32)
    seg0 = jnp.zeros_like(e12)
    for e in range(N_EXPERTS):
        seg0 = jnp.where(e12 == e, pad_start[e], seg0)
    dest = seg0 + r12
    nb_used = (seg_end[-1] // MOE_ROWS).astype(jnp.int32).reshape(1)
    blk = jnp.minimum(jnp.arange(n_blocks, dtype=jnp.int32), nb_used[0] - 1) * MOE_ROWS
    block_e = jnp.minimum(jnp.searchsorted(seg_end, blk, side="right"), N_EXPERTS - 1).astype(jnp.int32)

    lane = jnp.arange(MOE_ROWS, dtype=jnp.int32)[None, :]
    trash = 2 * n + block_e[:, None] * MOE_ROWS + lane
    slot_major = jnp.arange(2, dtype=jnp.int32)[None, :] * n + jnp.arange(n, dtype=jnp.int32)[:, None]
    _, by_row = lax.sort_key_val(dest.reshape(-1), slot_major.reshape(-1))
    by_row = jnp.concatenate([by_row, jnp.zeros((MOE_ROWS,), jnp.int32)])
    in_seg = jnp.arange(n_blocks, dtype=jnp.int32) - pad_start[block_e] // MOE_ROWS
    offset = (jnp.cumsum(cnt) - cnt)[block_e] + in_seg * MOE_ROWS
    n_valid = jnp.clip(cnt[block_e] - in_seg * MOE_ROWS, 0, MOE_ROWS)
    rows = jax.vmap(lambda o: lax.dynamic_slice(by_row, (o,), (MOE_ROWS,)))(offset)
    valid = lane < n_valid[:, None]
    row_dst = jnp.where(valid, rows, trash)
    row_tok = jnp.where(valid, rows % n, 0).reshape(n_blocks, 1, MOE_ROWS)
    dummy = 2 * n + N_EXPERTS * MOE_ROWS + lane
    row_dst = jnp.concatenate([dummy, row_dst]).reshape(n_blocks + 1, 1, MOE_ROWS)
    n_out_rows = 2 * n + (N_EXPERTS + 1) * MOE_ROWS
    y12 = _expert_swiglu(hx, row_tok, row_dst, block_e, nb_used, w1, w3, w2, n_out_rows)
    return _combine(x1, route, g2, final_g, y12, seq)


def _grid_pos_embed(rows, dim):
    row = jnp.broadcast_to(jnp.arange(rows, dtype=F32)[:, None], (rows, GRID_W)).reshape(-1)
    col = jnp.broadcast_to(jnp.arange(GRID_W, dtype=F32)[None, :], (rows, GRID_W)).reshape(-1)
    quarter = dim // 4
    omega = 1.0 / (10000.0 ** (jnp.arange(quarter, dtype=F32) / quarter))

    def enc(p):
        ang = p[:, None] * omega[None, :]
        return jnp.concatenate([jnp.sin(ang), jnp.cos(ang)], axis=-1)

    return jnp.concatenate([enc(row), enc(col)], axis=-1)


def _gate_lane_vec(t):
    v = jnp.zeros((2, 2 * GDN_HEADS), F32).at[:, :GDN_HEADS].set(t.astype(F32)).reshape(1, -1)
    return jnp.pad(v, ((0, 0), (0, GATE_PAD - v.shape[1])))


def kernel(x, c, ctx, c_ctx, w_mod, b_mod, norm1_g, norm2_g, w_in, conv_w, a_log, dt_bias, o_norm_g, sgu_norm_g, w_s, b_s, w_out, ffn_w1, ffn_w3, ffn_w2, router_w, router_b, exp_w1, exp_w3, exp_w2, final_g):
    bsz, seq, d = x.shape
    ctx_len = ctx.shape[1]
    depth = w_mod.shape[0]
    assert seq % TF == 0 and ctx_len % TM == 0 and (bsz * ctx_len) % TF == 0 and depth == 2
    n_lat = bsz * seq

    cond_rows = -(-(bsz + 1) // SUBLANES) * SUBLANES
    cond = jnp.zeros((cond_rows, d), F32).at[:bsz].set(c).at[bsz].set(c_ctx)
    mod = _modulation(cond, w_mod, b_mod)[:, :bsz + 1].reshape(depth, bsz + 1, 6, 1, d)

    pos_ext = jnp.concatenate([_grid_pos_embed(seq // GRID_W, d), jnp.zeros((TF, d), F32)], axis=0)
    xa = jnp.concatenate([x.reshape(n_lat, d), ctx.reshape(bsz * ctx_len, d)], axis=0)

    qkvz_w = 4 * GDN_WIDTH
    n_gate = 4 * GDN_HEADS
    out = None
    for layer in range(depth):
        sh1, sc1, g1, sh2, sc2, g2 = (mod[layer, :, i] for i in range(6))
        wl = w_in[layer]
        w_in_r = jnp.concatenate([wl[:, :qkvz_w], wl[:, qkvz_w + n_gate:], wl[:, qkvz_w:qkvz_w + n_gate],
                                  jnp.zeros((d, GATE_PAD - n_gate), F32)], axis=1).astype(BF16)
        b_s_b = jnp.broadcast_to(b_s[layer][:, :, None], (GMLP_GROUPS, GMLP_CHUNK, GMLP_GROUP_DIM))
        pos = pos_ext if layer == 0 else None
        qkv, zs, gates, mx = _in_stage(
            xa, pos, norm1_g[layer][None], sh1, sc1, w_in_r, conv_w[layer],
            _gate_lane_vec(a_log[layer]), _gate_lane_vec(dt_bias[layer]), sgu_norm_g[layer][None],
            w_s[layer].astype(BF16), b_s_b, bsz, seq, ctx_len)
        o_f, o_b = _gdn_scan(qkv, gates, bsz, seq, ctx_len)
        mix_specs, mix_args = _mix_inputs(xa, pos, o_f, o_b, zs, mx, o_norm_g[layer][None], g1,
                                          w_out[layer].astype(BF16), bsz, seq, ctx_len)
        i = layer // 2
        if layer % 2 == 0:
            xa = _mix_dense_ffn(mix_specs, mix_args, pos is not None, norm2_g[layer][None], sh2, sc2, g2,
                                ffn_w1[i].astype(BF16), ffn_w3[i].astype(BF16), ffn_w2[i].astype(BF16), bsz, seq)
        else:
            out = _moe_layer(mix_specs, mix_args, n_lat, norm2_g[layer][None], sh2, sc2, g2, router_w[i],
                             router_b[i], exp_w1[i].astype(BF16), exp_w3[i].astype(BF16),
                             exp_w2[i].astype(BF16), final_g[None], seq)
    return out.reshape(bsz, seq, d)
```
